```python
import jax, jax.numpy as jnp
from jax import lax
import numpy as np

D_MODEL = 2048
BATCH = 4
SEQ = 2048
DEPTH = 2
DEC_BATCH = 128
DEC_SEQ = 8
PAST_LEN = 16384
PAGE_SIZE = 128

GLA_HEADS = 4
GLA_DK = D_MODEL // 2 // GLA_HEADS
GLA_DV = D_MODEL // GLA_HEADS
GLA_GATE_RANK = 16
GLA_GATE_TAU = 16.0
GLA_CHUNK = 64
SSD_INNER = 2 * D_MODEL
SSD_HEAD_DIM = 64
SSD_HEADS = SSD_INNER // SSD_HEAD_DIM
SSD_GROUPS = 8
SSD_HPG = SSD_HEADS // SSD_GROUPS
SSD_STATE = 128
SSD_CONV = 4
SSD_CONV_DIM = SSD_INNER + 2 * SSD_GROUPS * SSD_STATE
SSD_CHUNK = 64
D_FF = ((8 * D_MODEL // 3 + 255) // 256) * 256
N_ADA = 6
EPS = 1e-6
SPLIT_SIZES = (GLA_HEADS * GLA_DK, GLA_HEADS * GLA_DK, GLA_HEADS * GLA_DV, GLA_HEADS * GLA_DV,
               GLA_GATE_RANK, SSD_INNER, SSD_CONV_DIM, SSD_HEADS, 2 * D_MODEL)
N_IN = sum(SPLIT_SIZES)

kernel_name = "hybrid_gla_ssd_adaln_decoder_step"


def rms_norm(x):
    xf = x.astype(jnp.float32)
    return (xf * lax.rsqrt(jnp.mean(xf * xf, axis=-1, keepdims=True) + EPS)).astype(x.dtype)


def _chunk(a, L, n):
    B, T = a.shape[:2]
    a = jnp.pad(a.astype(jnp.float32), [(0, 0), (0, n * L - T)] + [(0, 0)] * (a.ndim - 2))
    return jnp.moveaxis(a.reshape(B, n, L, *a.shape[2:]), 1, 0)


def gla_chunked(q, k, v, log_a, S0):
    B, T, H, _ = q.shape
    L = min(GLA_CHUNK, T)
    n = -(-T // L)
    qc, kc, vc, gc = [jnp.swapaxes(_chunk(a, L, n), 2, 3) for a in (q, k, v, log_a)]
    mask = jnp.tril(jnp.ones((L, L), dtype=bool))

    def step(S, inp):
        qi, ki, vi, gi = inp
        b = jnp.cumsum(gi, axis=2)
        q_ = qi * jnp.exp(b)
        k_ = ki * jnp.exp(-b)
        att = jnp.where(mask, jnp.einsum('bhtk,bhsk->bhts', q_, k_), 0.0)
        o = jnp.einsum('bhtk,bhkv->bhtv', q_, S) + jnp.einsum('bhts,bhsv->bhtv', att, vi)
        bL = b[:, :, -1:, :]
        S = jnp.exp(bL[:, :, 0, :])[..., None] * S + jnp.einsum('bhsk,bhsv->bhkv', ki * jnp.exp(bL - b), vi)
        return S, o

    S, o = lax.scan(step, S0.astype(jnp.float32), (qc, kc, vc, gc))
    o = jnp.moveaxis(jnp.swapaxes(o, 2, 3), 0, 1).reshape(B, n * L, H, GLA_DV)[:, :T]
    return o, S


def ssd_chunked(xh, dt, A, Bm, Cm, h0):
    Bsz, T = xh.shape[:2]
    L = min(SSD_CHUNK, T)
    n = -(-T // L)
    xc, dtc, Bc, Cc = [_chunk(a, L, n) for a in (xh, dt, Bm, Cm)]
    mask = jnp.tril(jnp.ones((L, L), dtype=bool))[None, :, :, None, None]

    def step(h, inp):
        xi, dti, Bi, Ci = inp
        cs = jnp.cumsum(dti * A, axis=1)
        seg = cs[:, :, None] - cs[:, None, :]
        decay = jnp.exp(jnp.where(mask, seg, -jnp.inf))
        cb = jnp.einsum('btgn,bsgn->btsg', Ci, Bi)
        w = cb[..., None] * decay * dti[:, None]
        y = jnp.einsum('btsgh,bsghp->btghp', w, xi)
        y = y + jnp.exp(cs)[..., None] * jnp.einsum('btgn,bghpn->btghp', Ci, h)
        dL = jnp.exp(cs[:, -1:] - cs) * dti
        h = jnp.exp(cs[:, -1])[..., None, None] * h + jnp.einsum('bsgh,bsghp,bsgn->bghpn', dL, xi, Bi)
        return h, y

    h, y = lax.scan(step, h0.astype(jnp.float32), (xc, dtc, Bc, Cc))
    y = jnp.moveaxis(y, 0, 1).reshape(Bsz, n * L, SSD_GROUPS, SSD_HPG, SSD_HEAD_DIM)[:, :T]
    return y, h


def causal_conv(u, buf, w, b):
    T = u.shape[1]
    full = jnp.concatenate([buf.astype(u.dtype), u], axis=1)
    out = b + sum(full[:, i:i + T] * w[i] for i in range(SSD_CONV))
    return out, full[:, -(SSD_CONV - 1):]


def mixer(h, S_gla, h_ssm, conv_buf, w_in, w_gla_gate, b_gla_gate, gla_norm, w_gla_proj,
          conv_w, conv_b, dt_bias, A_log, d_skip, ssd_norm, w_ssd_proj, w_mix_out):
    Bsz, T, _ = h.shape
    dtype = h.dtype
    proj = h @ w_in
    points, acc = [], 0
    for s in SPLIT_SIZES[:-1]:
        acc += s
        points.append(acc)
    q, k, v, r, glr, z, xbc, dt, gates = jnp.split(proj, points, axis=-1)
    q = q.reshape(Bsz, T, GLA_HEADS, GLA_DK) * (GLA_DK ** -0.5)
    k = k.reshape(Bsz, T, GLA_HEADS, GLA_DK)
    v = v.reshape(Bsz, T, GLA_HEADS, GLA_DV)
    log_a = jax.nn.log_sigmoid((glr @ w_gla_gate + b_gla_gate).astype(jnp.float32)) / GLA_GATE_TAU
    o, S_new = gla_chunked(q, k, v, log_a.reshape(Bsz, T, GLA_HEADS, GLA_DK), S_gla)
    o = (rms_norm(o) * gla_norm).reshape(Bsz, T, GLA_HEADS * GLA_DV).astype(dtype) * jax.nn.silu(r)
    y_a = o @ w_gla_proj
    xbc, conv_new = causal_conv(xbc, conv_buf, conv_w, conv_b)
    xbc = jax.nn.silu(xbc)
    xs, Bm, Cm = jnp.split(xbc, [SSD_INNER, SSD_INNER + SSD_GROUPS * SSD_STATE], axis=-1)
    xh = xs.reshape(Bsz, T, SSD_GROUPS, SSD_HPG, SSD_HEAD_DIM)
    Bm = Bm.reshape(Bsz, T, SSD_GROUPS, SSD_STATE)
    Cm = Cm.reshape(Bsz, T, SSD_GROUPS, SSD_STATE)
    dtp = jax.nn.softplus((dt + dt_bias).astype(jnp.float32)).reshape(Bsz, T, SSD_GROUPS, SSD_HPG)
    A = -jnp.exp(A_log.astype(jnp.float32)).reshape(SSD_GROUPS, SSD_HPG)
    y, h_new = ssd_chunked(xh, dtp, A, Bm, Cm,
                           h_ssm.reshape(Bsz, SSD_GROUPS, SSD_HPG, SSD_HEAD_DIM, SSD_STATE))
    y = y + d_skip.reshape(SSD_GROUPS, SSD_HPG)[..., None].astype(jnp.float32) * xh.astype(jnp.float32)
    y = y.reshape(Bsz, T, SSD_INNER) * jax.nn.silu(z.astype(jnp.float32))
    y = rms_norm(y.reshape(Bsz, T, SSD_GROUPS, SSD_INNER // SSD_GROUPS)).reshape(Bsz, T, SSD_INNER)
    y_b = (y * ssd_norm).astype(dtype) @ w_ssd_proj
    g_a, g_b = jnp.split(gates, 2, axis=-1)
    merged = jax.nn.sigmoid(g_a) * y_a + jax.nn.sigmoid(g_b) * y_b
    return (merged @ w_mix_out, S_new,
            h_new.reshape(Bsz, SSD_HEADS, SSD_HEAD_DIM, SSD_STATE), conv_new)


def trunk(x, c, S_gla, h_ssm, conv_buf, params):
    (w_ada, b_ada, w_in, w_gla_gate, b_gla_gate, gla_norm, w_gla_proj, conv_w, conv_b, dt_bias,
     A_log, d_skip, ssd_norm, w_ssd_proj, w_mix_out, w_ffn_in, w_ffn_out, final_norm) = params
    new_gla, new_ssm, new_conv = [], [], []
    for l in range(DEPTH):
        mod = (jax.nn.silu(c) @ w_ada[l] + b_ada[l])[:, None, :]
        sh1, sc1, g1, sh2, sc2, g2 = jnp.split(mod, N_ADA, axis=-1)
        h = rms_norm(x) * (1 + sc1) + sh1
        m, S, hs, cb = mixer(h, S_gla[l], h_ssm[l], conv_buf[l], w_in[l], w_gla_gate[l], b_gla_gate[l],
                             gla_norm[l], w_gla_proj[l], conv_w[l], conv_b[l], dt_bias[l], A_log[l],
                             d_skip[l], ssd_norm[l], w_ssd_proj[l], w_mix_out[l])
        x = x + g1 * m
        h = rms_norm(x) * (1 + sc2) + sh2
        gt, up = jnp.split(h @ w_ffn_in[l], 2, axis=-1)
        x = x + g2 * ((jax.nn.silu(gt) * up) @ w_ffn_out[l])
        new_gla.append(S)
        new_ssm.append(hs)
        new_conv.append(cb)
    y = rms_norm(x) * final_norm
    return y, jnp.stack(new_gla), jnp.stack(new_ssm), jnp.stack(new_conv)


def setup_inputs(seed: int = 0) -> dict:
    key = jax.random.key(seed)
    ks = jax.random.split(key, 32)
    f32 = jnp.float32
    nrm = lambda k, shape, s: jax.random.normal(k, shape, f32) * s
    u = jax.random.uniform(ks[20], (DEPTH, SSD_HEADS), f32)
    dt0 = jnp.exp(u * (jnp.log(0.1) - jnp.log(0.001)) + jnp.log(0.001))
    return {
        "x_prompt": nrm(ks[0], (BATCH, SEQ, D_MODEL), 1.0),
        "x_sample": nrm(ks[1], (DEC_BATCH, DEC_SEQ, D_MODEL), 1.0),
        "c_prompt": nrm(ks[2], (BATCH, D_MODEL), 1.0),
        "c_sample": nrm(ks[3], (DEC_BATCH, D_MODEL), 1.0),
        "state_gla": nrm(ks[4], (DEPTH, DEC_BATCH, GLA_HEADS, GLA_DK, GLA_DV), 0.5),
        "state_ssm": nrm(ks[5], (DEPTH, DEC_BATCH, SSD_HEADS, SSD_HEAD_DIM, SSD_STATE), 0.1),
        "state_conv": nrm(ks[6], (DEPTH, DEC_BATCH, SSD_CONV - 1, SSD_CONV_DIM), 1.0),
        "w_ada": nrm(ks[7], (DEPTH, D_MODEL, N_ADA * D_MODEL), D_MODEL ** -0.5),
        "b_ada": nrm(ks[8], (DEPTH, N_ADA * D_MODEL), 0.02),
        "w_in": nrm(ks[9], (DEPTH, D_MODEL, N_IN), D_MODEL ** -0.5),
        "w_gla_gate": nrm(ks[10], (DEPTH, GLA_GATE_RANK, GLA_HEADS * GLA_DK), GLA_GATE_RANK ** -0.5),
        "b_gla_gate": nrm(ks[11], (DEPTH, GLA_HEADS * GLA_DK), 0.1),
        "gla_norm": 1.0 + nrm(ks[12], (DEPTH, GLA_DV), 0.02),
        "w_gla_proj": nrm(ks[13], (DEPTH, GLA_HEADS * GLA_DV, D_MODEL), (GLA_HEADS * GLA_DV) ** -0.5),
        "conv_w": nrm(ks[14], (DEPTH, SSD_CONV, SSD_CONV_DIM), SSD_CONV ** -0.5),
        "conv_b": nrm(ks[15], (DEPTH, SSD_CONV_DIM), 0.02),
        "dt_bias": dt0 + jnp.log(-jnp.expm1(-dt0)),
        "A_log": jnp.log(jax.random.uniform(ks[16], (DEPTH, SSD_HEADS), f32, 1.0, 16.0)),
        "d_skip": 1.0 + nrm(ks[17], (DEPTH, SSD_HEADS), 0.02),
        "ssd_norm": 1.0 + nrm(ks[18], (DEPTH, SSD_INNER), 0.02),
        "w_ssd_proj": nrm(ks[19], (DEPTH, SSD_INNER, D_MODEL), SSD_INNER ** -0.5),
        "w_mix_out": nrm(ks[21], (DEPTH, D_MODEL, D_MODEL), D_MODEL ** -0.5),
        "w_ffn_in": nrm(ks[22], (DEPTH, D_MODEL, 2 * D_FF), D_MODEL ** -0.5),
        "w_ffn_out": nrm(ks[23], (DEPTH, D_FF, D_MODEL), D_FF ** -0.5),
        "final_norm": 1.0 + nrm(ks[24], (D_MODEL,), 0.02),
    }


def reference(x_prompt, x_sample, c_prompt, c_sample, state_gla, state_ssm, state_conv,
              w_ada, b_ada, w_in, w_gla_gate, b_gla_gate, gla_norm, w_gla_proj, conv_w, conv_b,
              dt_bias, A_log, d_skip, ssd_norm, w_ssd_proj, w_mix_out, w_ffn_in, w_ffn_out, final_norm):
    params = (w_ada, b_ada, w_in, w_gla_gate, b_gla_gate, gla_norm, w_gla_proj, conv_w, conv_b, dt_bias,
              A_log, d_skip, ssd_norm, w_ssd_proj, w_mix_out, w_ffn_in, w_ffn_out, final_norm)
    bp = x_prompt.shape[0]
    zero_gla = jnp.zeros((DEPTH, bp, GLA_HEADS, GLA_DK, GLA_DV), jnp.float32)
    zero_ssm = jnp.zeros((DEPTH, bp, SSD_HEADS, SSD_HEAD_DIM, SSD_STATE), jnp.float32)
    zero_conv = jnp.zeros((DEPTH, bp, SSD_CONV - 1, SSD_CONV_DIM), x_prompt.dtype)
    y_prompt, gla_p, ssm_p, conv_p = trunk(x_prompt, c_prompt, zero_gla, zero_ssm, zero_conv, params)
    y_sample, gla_s, ssm_s, conv_s = trunk(x_sample, c_sample, state_gla, state_ssm, state_conv, params)
    return (y_prompt, y_sample, gla_p, ssm_p, conv_p, gla_s, ssm_s, conv_s)
```

```python
import dataclasses
import functools

import jax
import jax.numpy as jnp
from jax import lax
from jax.experimental import pallas as pl
from jax.experimental.pallas import tpu as pltpu

F32 = jnp.float32
BF16 = jnp.bfloat16

LANE = 128
SUBLANE = 8
VMEM_LIMIT = 56 * 1024 * 1024
EPS = 1e-6
N_ADA = 6


@dataclasses.dataclass(frozen=True)
class Cfg:
    d_model: int = 2048
    depth: int = 2
    gla_heads: int = 4
    gla_rank: int = 16
    gla_tau: float = 16.0
    gla_chunk: int = 64
    ssd_head_dim: int = 64
    ssd_groups: int = 8
    ssd_state: int = 128
    ssd_conv: int = 4
    ssd_chunk: int = 64

    @property
    def dk(self):
        return self.d_model // 2 // self.gla_heads

    @property
    def dv(self):
        return self.d_model // self.gla_heads

    @property
    def inner(self):
        return 2 * self.d_model

    @property
    def ssd_heads(self):
        return self.inner // self.ssd_head_dim

    @property
    def hpg(self):
        return self.ssd_heads // self.ssd_groups

    @property
    def gw(self):
        return self.inner // self.ssd_groups

    @property
    def conv_dim(self):
        return self.inner + 2 * self.ssd_groups * self.ssd_state

    @property
    def d_ff(self):
        return ((8 * self.d_model // 3 + 255) // 256) * 256

    @property
    def off_q(self):
        return 0

    @property
    def off_k(self):
        return self.gla_heads * self.dk

    @property
    def off_v(self):
        return 2 * self.gla_heads * self.dk

    @property
    def off_r(self):
        return self.off_v + self.gla_heads * self.dv

    @property
    def off_z(self):
        return self.off_r + self.gla_heads * self.dv

    @property
    def off_xbc(self):
        return self.off_z + self.inner

    @property
    def off_gates(self):
        return self.off_xbc + self.conv_dim

    @property
    def off_glr(self):
        return self.off_gates + 2 * self.d_model

    @property
    def off_dt(self):
        return self.off_glr + LANE

    def n_packed(self, tn):
        n = self.off_dt + LANE
        return -(-n // tn) * tn


def _cparams(sem):
    return pltpu.CompilerParams(dimension_semantics=sem, vmem_limit_bytes=VMEM_LIMIT)


def _sigmoid(x):
    return 1.0 / (1.0 + jnp.exp(-x))


def _silu(x):
    return x * _sigmoid(x)


def _softplus(x):
    return jnp.maximum(x, 0.0) + jnp.log1p(jnp.exp(-jnp.abs(x)))


def _log_sigmoid(x):
    return jnp.minimum(x, 0.0) - jnp.log1p(jnp.exp(-jnp.abs(x)))


def _dot(a, b):
    return jnp.dot(a, b, preferred_element_type=F32)


def _dot_nt(a, b):
    return lax.dot_general(a, b, (((1,), (1,)), ((), ())), preferred_element_type=F32)


def _dot_tn(a, b):
    return lax.dot_general(a, b, (((0,), (0,)), ((), ())), preferred_element_type=F32)


def _split(x, parts):
    out = []
    r = x
    for i in range(parts):
        p = r.astype(BF16)
        out.append(p)
        if i + 1 < parts:
            r = r - p.astype(F32)
    return out


def _rms(x):
    return x * lax.rsqrt(jnp.mean(x * x, axis=-1, keepdims=True) + EPS)


def _expand_rows(v, reps):
    if reps == 1:
        return v
    g = v.shape[0]
    r = lax.broadcasted_iota(jnp.int32, (g * reps, g), 0)
    c = lax.broadcasted_iota(jnp.int32, (g * reps, g), 1)
    lo = c * reps
    e = jnp.where((r >= lo) & (r < lo + reps), 1.0, 0.0).astype(BF16)
    acc = None
    for p in _split(v, 3):
        d = _dot(e, p)
        acc = d if acc is None else acc + d
    return acc


def _tri(n, strict=False, upper=False):
    r = lax.broadcasted_iota(jnp.int32, (n, n), 0)
    c = lax.broadcasted_iota(jnp.int32, (n, n), 1)
    if upper:
        r, c = c, r
    return (r > c) if strict else (r >= c)


def _ada_kernel(c_ref, w_ref, b_ref, o_ref):
    a = _silu(c_ref[...]).astype(BF16)
    o_ref[...] = _dot(a, w_ref[...].astype(BF16)) + b_ref[...]


def _prenorm_kernel(x_ref, sc_ref, sh_ref, o_ref, *, reps):
    n = _rms(x_ref[...])
    sc = _expand_rows(sc_ref[...], reps)
    sh = _expand_rows(sh_ref[...], reps)
    o_ref[...] = (n * (1.0 + sc) + sh).astype(o_ref.dtype)


def _final_norm_kernel(x_ref, g_ref, o_ref):
    o_ref[...] = _rms(x_ref[...]) * g_ref[...]


def _mm_kernel(a_ref, w_ref, o_ref):
    o_ref[...] = _dot(a_ref[...], w_ref[...]).astype(o_ref.dtype)


def _merge_kernel(oa_ref, yb_ref, wa_ref, wb_ref, ga_ref, gb_ref, o_ref):
    ya = _dot(oa_ref[...].astype(BF16), wa_ref[...])
    yb = _dot(yb_ref[...].astype(BF16), wb_ref[...])
    o_ref[...] = (_sigmoid(ga_ref[...]) * ya + _sigmoid(gb_ref[...]) * yb).astype(o_ref.dtype)


def _resid_kernel(a_ref, w_ref, x_ref, g_ref, o_ref, *, reps):
    y = _dot(a_ref[...], w_ref[...])
    o_ref[...] = x_ref[...] + _expand_rows(g_ref[...], reps) * y


def _ffn_in_kernel(h_ref, wg_ref, wu_ref, o_ref):
    h = h_ref[...]
    o_ref[...] = (_silu(_dot(h, wg_ref[...])) * _dot(h, wu_ref[...])).astype(o_ref.dtype)


def _gla_kernel(*refs, L, nc, nt, zero_init, has_prev, scale, inv_tau):
    q_ref, k_ref, v_ref, glr_ref, r_ref, wg_ref, bg_ref, gn_ref = refs[:8]
    i = 8
    s0_ref = None
    if not zero_init:
        s0_ref = refs[i]
        i += 1
    if has_prev:
        i += 1
    o_ref, s_out_ref, s_scr = refs[i:i + 3]
    t = pl.program_id(2)
    dk, dv = s_scr.shape

    @pl.when(t == 0)
    def _init():
        if zero_init:
            s_scr[...] = jnp.zeros_like(s_scr)
        else:
            s_scr[...] = s0_ref[...]

    ga = _dot(glr_ref[...].astype(BF16), wg_ref[...]) + bg_ref[...]
    log_a = _log_sigmoid(ga) * inv_tau
    tri = jnp.where(_tri(L), 1.0, 0.0).astype(BF16)
    mask = _tri(L)
    ones = jnp.ones((L, LANE), BF16)
    for c in range(nc):
        sl = slice(c * L, (c + 1) * L)
        g_parts = _split(log_a[sl], 2)
        b = _dot(tri, g_parts[0]) + _dot(tri, g_parts[1])
        tot = _dot_tn(g_parts[0], ones) + _dot_tn(g_parts[1], ones)
        q = q_ref[sl, :] * scale
        k = k_ref[sl, :]
        v = v_ref[sl, :].astype(BF16)
        q_in = (q * jnp.exp(b)).astype(BF16)
        k_in = (k * jnp.exp(-b)).astype(BF16)
        k_out = (k * jnp.exp(b[L - 1:L, :] - b)).astype(BF16)
        att = jnp.where(mask, _dot_nt(q_in, k_in), 0.0)
        s = s_scr[...]
        o = _dot(q_in, s.astype(BF16)) + _dot(att.astype(BF16), v)
        decay = jnp.exp(tot)
        s_scr[...] = s * jnp.concatenate([decay] * (dv // LANE), axis=1) + _dot_tn(k_out, v)
        on = _rms(o) * gn_ref[...]
        o_ref[sl, :] = (on * _silu(r_ref[sl, :])).astype(o_ref.dtype)

    @pl.when(t == nt - 1)
    def _fin():
        s_out_ref[...] = s_scr[...]


def _ssd_kernel(*refs, L, nc, nt, zero_init, has_prev, hpg, P):
    (z_ref, x_ref, b_ref, c_ref, dt_ref, cwx_ref, cwb_ref, cwc_ref, cbx_ref, cbb_ref, cbc_ref,
     dtb_ref, alog_ref, dsk_ref, nrm_ref) = refs[:15]
    i = 15
    csx_ref = csb_ref = csc_ref = h0_ref = None
    if not zero_init:
        csx_ref, csb_ref, csc_ref, h0_ref = refs[i:i + 4]
        i += 4
    if has_prev:
        i += 1
    y_ref, h_out_ref, ux, ub, uc, h_scr = refs[i:i + 6]
    g = pl.program_id(1)
    t = pl.program_id(2)
    Tb = L * nc
    gw = hpg * P
    nconv = cwx_ref.shape[0]
    halo = SUBLANE

    @pl.when(t == 0)
    def _init():
        for u, cs in ((ux, csx_ref), (ub, csb_ref), (uc, csc_ref)):
            u[0:halo, :] = jnp.zeros((halo, u.shape[1]), F32)
            if not zero_init:
                u[halo - (nconv - 1):halo, :] = cs[...]
        if zero_init:
            h_scr[...] = jnp.zeros_like(h_scr)
        else:
            h_scr[...] = h0_ref[...]

    def conv(u, src_ref, cw_ref, cb_ref):
        u[halo:halo + Tb, :] = src_ref[...]
        acc = cb_ref[...]
        for j in range(nconv):
            acc = acc + u[pl.ds(halo - (nconv - 1) + j, Tb), :] * cw_ref[j:j + 1, :]
        u[0:halo, :] = u[Tb:Tb + halo, :]
        return _silu(acc)

    xc = conv(ux, x_ref, cwx_ref, cbx_ref)
    bc = conv(ub, b_ref, cwb_ref, cbb_ref)
    cc = conv(uc, c_ref, cwc_ref, cbc_ref)

    dtp = _softplus(dt_ref[...] + dtb_ref[...])
    a_h = dtp * (-jnp.exp(alog_ref[...]))

    def expander(width, per):
        j = lax.broadcasted_iota(jnp.int32, (LANE, width), 0) - g * hpg
        col = lax.broadcasted_iota(jnp.int32, (LANE, width), 1)
        lo = j * per
        return jnp.where((col >= lo) & (col < lo + per), 1.0, 0.0).astype(BF16)

    def expand(val, ex):
        p = _split(val, 2)
        return _dot(p[0], ex) + _dot(p[1], ex)

    ex_p = expander(gw, P)
    dt_exp = expand(dtp, ex_p)
    a_exp_p = expand(a_h, ex_p)
    a_exp_l = a_exp_p if L == P else expand(a_h, expander(hpg * L, L))

    tri = jnp.where(_tri(L), 1.0, 0.0).astype(BF16)
    ones = jnp.ones((L, LANE), BF16)
    wl = hpg * L
    row_l = lax.broadcasted_iota(jnp.int32, (L, wl), 0)
    s_l = lax.broadcasted_iota(jnp.int32, (L, wl), 1) % L
    later = row_l > s_l
    causal = row_l >= s_l
    bd_r = lax.broadcasted_iota(jnp.int32, (wl, gw), 0) // L
    bd_c = lax.broadcasted_iota(jnp.int32, (wl, gw), 1) // P
    blockdiag = bd_r == bd_c

    for c in range(nc):
        sl = slice(c * L, (c + 1) * L)
        x = xc[sl]
        b = bc[sl]
        cm = cc[sl].astype(BF16)
        ap = _split(a_exp_p[sl], 2)
        al = _split(jnp.where(later, a_exp_l[sl], 0.0), 2)
        ccol = _dot(tri, ap[0]) + _dot(tri, ap[1])
        seg = _dot(tri, al[0]) + _dot(tri, al[1])
        decay = jnp.where(causal, jnp.exp(seg), 0.0)
        b_t = jnp.concatenate([b] * hpg, axis=0).astype(BF16)
        w_cat = (_dot_nt(cm, b_t) * decay).astype(BF16)
        xt = x * dt_exp[sl]
        x_bd = jnp.where(blockdiag, jnp.concatenate([xt] * hpg, axis=0), 0.0).astype(BF16)
        h = h_scr[...]
        y = _dot(w_cat, x_bd) + jnp.exp(ccol) * _dot_nt(cm, h.astype(BF16))
        y = y + dsk_ref[...] * x
        x_dl = (xt * jnp.exp(ccol[L - 1:L, :] - ccol)).astype(BF16)
        tot = _dot_tn(ap[0], ones) + _dot_tn(ap[1], ones)
        n_rep = h.shape[1] // LANE
        h_scr[...] = h * jnp.concatenate([jnp.exp(tot)] * n_rep, axis=1) + _dot_tn(x_dl, b.astype(BF16))
        yg = y * _silu(z_ref[sl, :])
        y_ref[sl, :] = (_rms(yg) * nrm_ref[...]).astype(y_ref.dtype)

    @pl.when(t == nt - 1)
    def _fin():
        h_out_ref[...] = h_scr[...]


def _ada(c_all, w_ada, b_ada, tn=512):
    depth, d, n = w_ada.shape
    m = c_all.shape[0]
    return pl.pallas_call(
        _ada_kernel,
        grid=(depth, n // tn),
        in_specs=[
            pl.BlockSpec((m, d), lambda l, j: (0, 0)),
            pl.BlockSpec((None, d, tn), lambda l, j: (l, 0, j)),
            pl.BlockSpec((None, 1, tn), lambda l, j: (l, 0, j)),
        ],
        out_specs=pl.BlockSpec((None, m, tn), lambda l, j: (l, 0, j)),
        out_shape=jax.ShapeDtypeStruct((depth, m, n), F32),
        compiler_params=_cparams(("parallel", "parallel")),
        name="ada",
    )(c_all, w_ada, b_ada.reshape(depth, 1, n))


class _Rows:
    def __init__(self, n_seq, seq_len, long_seq, tm_max=1024):
        self.n_seq, self.seq_len, self.long_seq = n_seq, seq_len, long_seq
        self.m = n_seq * seq_len
        if long_seq:
            self.tm = min(tm_max, seq_len)
            assert seq_len % self.tm == 0
            self.reps = 1
            self.mod_rows = 1
        else:
            self.tm = min(self.m, tm_max)
            assert self.m % self.tm == 0 and self.tm % seq_len == 0
            self.reps = seq_len
            self.mod_rows = self.tm // seq_len
        self.n_tiles = self.m // self.tm

    def lhs_spec(self, width):
        if self.n_tiles == 1:
            return pl.BlockSpec((self.tm, width), lambda i, j: (i, 0), pipeline_mode=pl.Buffered(1))
        return pl.BlockSpec((self.tm, width), lambda i, j: (i, 0))

    def mod_array(self, mod):
        depth, _, n = mod.shape
        if self.reps == 1:
            return mod.reshape(depth, self.n_seq, 1, n)
        return mod

    def mod_spec(self, l, tn, col_block):
        if self.reps == 1:
            per = self.seq_len // self.tm
            return pl.BlockSpec((None, None, 1, tn), lambda i, j: (l, i // per, 0, col_block(j)))
        return pl.BlockSpec((None, self.mod_rows, tn), lambda i, j: (l, i, col_block(j)))


def _prenorm(x, mod_arr, rows, l, v_scale, v_shift):
    m, d = x.shape
    return pl.pallas_call(
        functools.partial(_prenorm_kernel, reps=rows.reps),
        grid=(rows.n_tiles, 1),
        in_specs=[
            pl.BlockSpec((rows.tm, d), lambda i, j: (i, 0)),
            rows.mod_spec(l, d, lambda j: v_scale),
            rows.mod_spec(l, d, lambda j: v_shift),
        ],
        out_specs=pl.BlockSpec((rows.tm, d), lambda i, j: (i, 0)),
        out_shape=jax.ShapeDtypeStruct((m, d), BF16),
        compiler_params=_cparams(("parallel", "arbitrary")),
        name="prenorm",
    )(x, mod_arr, mod_arr)


def _final_norm(x, gain, rows):
    m, d = x.shape
    return pl.pallas_call(
        _final_norm_kernel,
        grid=(rows.n_tiles,),
        in_specs=[pl.BlockSpec((rows.tm, d), lambda i: (i, 0)), pl.BlockSpec((1, d), lambda i: (0, 0))],
        out_specs=pl.BlockSpec((rows.tm, d), lambda i: (i, 0)),
        out_shape=jax.ShapeDtypeStruct((m, d), F32),
        compiler_params=_cparams(("parallel",)),
        name="final_norm",
    )(x, gain.reshape(1, d))


def _in_proj(h, w, l, rows, tn):
    m, d = h.shape
    n = w.shape[2]
    return pl.pallas_call(
        _mm_kernel,
        grid=(rows.n_tiles, n // tn),
        in_specs=[
            rows.lhs_spec(d),
            pl.BlockSpec((None, d, tn), lambda i, j: (l, 0, j)),
        ],
        out_specs=pl.BlockSpec((rows.tm, tn), lambda i, j: (i, j)),
        out_shape=jax.ShapeDtypeStruct((m, n), F32),
        compiler_params=_cparams(("parallel", "arbitrary")),
        name="in_proj",
    )(h, w)


def _merge(cfg, oa, yb, proj, wa, wb, l, rows, tn=256):
    m = oa.shape[0]
    d = cfg.d_model
    ga0 = cfg.off_gates // tn
    gb0 = (cfg.off_gates + d) // tn
    return pl.pallas_call(
        _merge_kernel,
        grid=(rows.n_tiles, d // tn),
        in_specs=[
            rows.lhs_spec(oa.shape[1]),
            rows.lhs_spec(yb.shape[1]),
            pl.BlockSpec((None, wa.shape[1], tn), lambda i, j: (l, 0, j)),
            pl.BlockSpec((None, wb.shape[1], tn), lambda i, j: (l, 0, j)),
            pl.BlockSpec((rows.tm, tn), lambda i, j: (i, ga0 + j)),
            pl.BlockSpec((rows.tm, tn), lambda i, j: (i, gb0 + j)),
        ],
        out_specs=pl.BlockSpec((rows.tm, tn), lambda i, j: (i, j)),
        out_shape=jax.ShapeDtypeStruct((m, d), BF16),
        compiler_params=_cparams(("parallel", "arbitrary")),
        name="merge",
    )(oa, yb, wa, wb, proj, proj)


def _resid(a, w, x, mod_arr, rows, l, v_gate, tn=256, name="resid"):
    m, kdim = a.shape
    d = x.shape[1]
    per = d // tn
    return pl.pallas_call(
        functools.partial(_resid_kernel, reps=rows.reps),
        grid=(rows.n_tiles, d // tn),
        in_specs=[
            rows.lhs_spec(kdim),
            pl.BlockSpec((None, kdim, tn), lambda i, j: (l, 0, j)),
            pl.BlockSpec((rows.tm, tn), lambda i, j: (i, j)),
            rows.mod_spec(l, tn, lambda j: v_gate * per + j),
        ],
        out_specs=pl.BlockSpec((rows.tm, tn), lambda i, j: (i, j)),
        out_shape=jax.ShapeDtypeStruct((m, d), F32),
        compiler_params=_cparams(("parallel", "arbitrary")),
        name=name,
    )(a, w, x, mod_arr)


def _ffn_in(h, w, l, rows, d_ff, tn=512):
    m, d = h.shape
    nb = d_ff // tn
    return pl.pallas_call(
        _ffn_in_kernel,
        grid=(rows.n_tiles, nb),
        in_specs=[
            rows.lhs_spec(d),
            pl.BlockSpec((None, d, tn), lambda i, j: (l, 0, j)),
            pl.BlockSpec((None, d, tn), lambda i, j: (l, 0, nb + j)),
        ],
        out_specs=pl.BlockSpec((rows.tm, tn), lambda i, j: (i, j)),
        out_shape=jax.ShapeDtypeStruct((m, d_ff), BF16),
        compiler_params=_cparams(("parallel", "arbitrary")),
        name="ffn_in",
    )(h, w, w)


def _scan_blocks(cfg, rows, chunk):
    L = min(chunk, rows.seq_len)
    assert rows.seq_len % L == 0
    n_chunks = rows.seq_len // L
    nc = min(4, n_chunks)
    assert n_chunks % nc == 0
    return L, nc, n_chunks // nc


def _gla(cfg, proj, l, rows, wg, bg, gn, state, prev, out_dtype):
    H, dk, dv = cfg.gla_heads, cfg.dk, cfg.dv
    B = rows.n_seq
    L, nc, nt = _scan_blocks(cfg, rows, cfg.gla_chunk)
    Tb = L * nc
    rb = lambda b, t: b * nt + t
    in_specs = [
        pl.BlockSpec((Tb, dk), lambda b, h, t: (rb(b, t), cfg.off_q // dk + h)),
        pl.BlockSpec((Tb, dk), lambda b, h, t: (rb(b, t), cfg.off_k // dk + h)),
        pl.BlockSpec((Tb, dv), lambda b, h, t: (rb(b, t), cfg.off_v // dv + h)),
        pl.BlockSpec((Tb, LANE), lambda b, h, t: (rb(b, t), cfg.off_glr // LANE)),
        pl.BlockSpec((Tb, dv), lambda b, h, t: (rb(b, t), cfg.off_r // dv + h)),
        pl.BlockSpec((None, LANE, dk), lambda b, h, t: (l, 0, h)),
        pl.BlockSpec((None, 1, dk), lambda b, h, t: (l, 0, h)),
        pl.BlockSpec((None, 1, dv), lambda b, h, t: (l, 0, 0)),
    ]
    args = [proj, proj, proj, proj, proj, wg, bg, gn]
    if state is not None:
        in_specs.append(pl.BlockSpec((None, None, None, dk, dv), lambda b, h, t: (l, b, h, 0, 0)))
        args.append(state)
    aliases = {}
    if prev is not None:
        in_specs.append(pl.BlockSpec(memory_space=pl.ANY))
        aliases = {len(args): 1}
        args.append(prev)
    kern = functools.partial(_gla_kernel, L=L, nc=nc, nt=nt, zero_init=state is None,
                             has_prev=prev is not None, scale=dk ** -0.5, inv_tau=1.0 / cfg.gla_tau)
    return pl.pallas_call(
        kern,
        grid=(B, H, nt),
        in_specs=in_specs,
        out_specs=[
            pl.BlockSpec((Tb, dv), lambda b, h, t: (rb(b, t), h)),
            pl.BlockSpec((None, None, None, dk, dv), lambda b, h, t: (l, b, h, 0, 0)),
        ],
        out_shape=[
            jax.ShapeDtypeStruct((rows.m, H * dv), out_dtype),
            jax.ShapeDtypeStruct((cfg.depth, B, H, dk, dv), F32),
        ],
        scratch_shapes=[pltpu.VMEM((dk, dv), F32)],
        input_output_aliases=aliases,
        compiler_params=_cparams(("parallel", "parallel", "arbitrary")),
        name="gla_scan",
    )(*args)


def _ssd(cfg, proj, l, rows, conv_w, conv_b, dtb, alog, dsk, nrm, conv_state, state, prev, out_dtype):
    G, gw, N, P, hpg = cfg.ssd_groups, cfg.gw, cfg.ssd_state, cfg.ssd_head_dim, cfg.hpg
    B = rows.n_seq
    K = cfg.ssd_conv
    L, nc, nt = _scan_blocks(cfg, rows, cfg.ssd_chunk)
    Tb = L * nc
    rb = lambda b, t: b * nt + t
    xb0 = cfg.off_xbc // gw
    bb0 = (cfg.off_xbc + cfg.inner) // N
    cb0 = bb0 + G
    in_specs = [
        pl.BlockSpec((Tb, gw), lambda b, g, t: (rb(b, t), cfg.off_z // gw + g)),
        pl.BlockSpec((Tb, gw), lambda b, g, t: (rb(b, t), xb0 + g)),
        pl.BlockSpec((Tb, N), lambda b, g, t: (rb(b, t), bb0 + g)),
        pl.BlockSpec((Tb, N), lambda b, g, t: (rb(b, t), cb0 + g)),
        pl.BlockSpec((Tb, LANE), lambda b, g, t: (rb(b, t), cfg.off_dt // LANE)),
        pl.BlockSpec((None, K, gw), lambda b, g, t: (l, 0, g)),
        pl.BlockSpec((None, K, N), lambda b, g, t: (l, 0, cfg.inner // N + g)),
        pl.BlockSpec((None, K, N), lambda b, g, t: (l, 0, cfg.inner // N + G + g)),
        pl.BlockSpec((None, 1, gw), lambda b, g, t: (l, 0, g)),
        pl.BlockSpec((None, 1, N), lambda b, g, t: (l, 0, cfg.inner // N + g)),
        pl.BlockSpec((None, 1, N), lambda b, g, t: (l, 0, cfg.inner // N + G + g)),
        pl.BlockSpec((None, 1, LANE), lambda b, g, t: (l, 0, 0)),
        pl.BlockSpec((None, 1, LANE), lambda b, g, t: (l, 0, 0)),
        pl.BlockSpec((None, 1, gw), lambda b, g, t: (l, 0, g)),
        pl.BlockSpec((None, 1, gw), lambda b, g, t: (l, 0, g)),
    ]
    args = [proj, proj, proj, proj, proj, conv_w, conv_w, conv_w, conv_b, conv_b, conv_b, dtb, alog, dsk, nrm]
    if state is not None:
        in_specs += [
            pl.BlockSpec((None, None, K - 1, gw), lambda b, g, t: (l, b, 0, g)),
            pl.BlockSpec((None, None, K - 1, N), lambda b, g, t: (l, b, 0, cfg.inner // N + g)),
            pl.BlockSpec((None, None, K - 1, N), lambda b, g, t: (l, b, 0, cfg.inner // N + G + g)),
            pl.BlockSpec((None, None, None, gw, N), lambda b, g, t: (l, b, g, 0, 0)),
        ]
        args += [conv_state, conv_state, conv_state, state]
    aliases = {}
    if prev is not None:
        in_specs.append(pl.BlockSpec(memory_space=pl.ANY))
        aliases = {len(args): 1}
        args.append(prev)
    kern = functools.partial(_ssd_kernel, L=L, nc=nc, nt=nt, zero_init=state is None,
                             has_prev=prev is not None, hpg=hpg, P=P)
    return pl.pallas_call(
        kern,
        grid=(B, G, nt),
        in_specs=in_specs,
        out_specs=[
            pl.BlockSpec((Tb, gw), lambda b, g, t: (rb(b, t), g)),
            pl.BlockSpec((None, None, None, gw, N), lambda b, g, t: (l, b, g, 0, 0)),
        ],
        out_shape=[
            jax.ShapeDtypeStruct((rows.m, cfg.inner), out_dtype),
            jax.ShapeDtypeStruct((cfg.depth, B, G, gw, N), F32),
        ],
        scratch_shapes=[
            pltpu.VMEM((Tb + SUBLANE, gw), F32),
            pltpu.VMEM((Tb + SUBLANE, N), F32),
            pltpu.VMEM((Tb + SUBLANE, N), F32),
            pltpu.VMEM((gw, N), F32),
        ],
        input_output_aliases=aliases,
        compiler_params=_cparams(("parallel", "parallel", "arbitrary")),
        name="ssd_scan",
    )(*args)


def _pack_w_in(cfg, w_in, tn):
    H, dk, dv = cfg.gla_heads, cfg.dk, cfg.dv
    sizes = (H * dk, H * dk, H * dv, H * dv, cfg.gla_rank, cfg.inner, cfg.conv_dim, cfg.ssd_heads, 2 * cfg.d_model)
    offs = [0]
    for s in sizes:
        offs.append(offs[-1] + s)
    seg = lambda i: w_in[:, :, offs[i]:offs[i + 1]]
    padto = lambda a, n: jnp.pad(a, ((0, 0), (0, 0), (0, n - a.shape[2])))
    n_p = cfg.n_packed(tn)
    parts = [seg(0), seg(1), seg(2), seg(3), seg(5), seg(6), seg(8), padto(seg(4), LANE), padto(seg(7), LANE)]
    packed = jnp.concatenate(parts, axis=2)
    return padto(packed, n_p).astype(BF16)


def _trunk(cfg, x, mod, rows, states, weights, scan_dtype):
    (w_in_p, wg, bg, gn, w_gla_proj, conv_w, conv_b, dtb, alog, dsk, nrm, w_ssd_proj, w_mix_out,
     w_ffn_in, w_ffn_out, final_norm, tn_in) = weights
    s_gla, s_ssm, s_conv = states
    mod_arr = rows.mod_array(mod)
    rows_e = _Rows(rows.n_seq, rows.seq_len, rows.long_seq, tm_max=256)
    B = rows.n_seq
    K = cfg.ssd_conv
    new_gla = new_ssm = None
    new_conv = []
    for l in range(cfg.depth):
        h = _prenorm(x, mod_arr, rows_e, l, 1, 0)
        proj = _in_proj(h, w_in_p, l, rows, tn_in)
        oa, new_gla = _gla(cfg, proj, l, rows, wg, bg, gn, s_gla, new_gla, scan_dtype)
        yb, new_ssm = _ssd(cfg, proj, l, rows, conv_w, conv_b, dtb, alog, dsk, nrm, s_conv, s_ssm, new_ssm,
                           scan_dtype)
        assert rows.seq_len >= K - 1
        xbc = proj[:, cfg.off_xbc:cfg.off_xbc + cfg.conv_dim].reshape(B, rows.seq_len, cfg.conv_dim)
        new_conv.append(xbc[:, rows.seq_len - (K - 1):, :])
        merged = _merge(cfg, oa, yb, proj, w_gla_proj, w_ssd_proj, l, rows)
        x = _resid(merged, w_mix_out, x, mod_arr, rows, l, 2, name="mix_out")
        h = _prenorm(x, mod_arr, rows_e, l, 4, 3)
        act = _ffn_in(h, w_ffn_in, l, rows, cfg.d_ff)
        x = _resid(act, w_ffn_out, x, mod_arr, rows, l, 5, name="ffn_out")
    y = _final_norm(x, final_norm, rows_e)
    return y, new_gla, new_ssm, jnp.stack(new_conv)


def _forward(cfg, x_prompt, x_sample, c_prompt, c_sample, state_gla, state_ssm, state_conv,
             w_ada, b_ada, w_in, w_gla_gate, b_gla_gate, gla_norm, w_gla_proj, conv_w, conv_b,
             dt_bias, A_log, d_skip, ssd_norm, w_ssd_proj, w_mix_out, w_ffn_in, w_ffn_out, final_norm):
    depth, d = cfg.depth, cfg.d_model
    bp, tp, _ = x_prompt.shape
    bs, ts, _ = x_sample.shape
    H, dk, dv = cfg.gla_heads, cfg.dk, cfg.dv
    G, gw, N, P = cfg.ssd_groups, cfg.gw, cfg.ssd_state, cfg.ssd_head_dim
    assert cfg.ssd_heads <= LANE and cfg.gla_rank <= LANE
    assert cfg.off_v % dv == 0 and cfg.off_z % gw == 0 and cfg.off_xbc % gw == 0 and cfg.off_gates % 256 == 0

    pad = (-bp) % SUBLANE
    c_all = jnp.concatenate([c_prompt, jnp.zeros((pad, d), F32), c_sample], axis=0)
    mod = _ada(c_all, w_ada, b_ada)
    mod_p = mod[:, :bp]
    mod_s = mod[:, bp + pad:]

    tn_in = 512
    padl = lambda a: jnp.pad(a, ((0, 0), (0, LANE - a.shape[1])))
    weights = (
        _pack_w_in(cfg, w_in, tn_in),
        jnp.pad(w_gla_gate, ((0, 0), (0, LANE - cfg.gla_rank), (0, 0))).astype(BF16),
        b_gla_gate.reshape(depth, 1, H * dk),
        gla_norm.reshape(depth, 1, dv),
        w_gla_proj.astype(BF16),
        conv_w,
        conv_b.reshape(depth, 1, cfg.conv_dim),
        padl(dt_bias).reshape(depth, 1, LANE),
        padl(A_log).reshape(depth, 1, LANE),
        jnp.repeat(d_skip, P, axis=1).reshape(depth, 1, cfg.inner),
        ssd_norm.reshape(depth, 1, cfg.inner),
        w_ssd_proj.astype(BF16),
        w_mix_out.astype(BF16),
        w_ffn_in.astype(BF16),
        w_ffn_out.astype(BF16),
        final_norm,
        tn_in,
    )
    rows_p = _Rows(bp, tp, True)
    rows_s = _Rows(bs, ts, False)
    y_p, gla_p, ssm_p, conv_p = _trunk(cfg, x_prompt.reshape(bp * tp, d), mod_p, rows_p,
                                       (None, None, None), weights, BF16)
    states_s = (state_gla, state_ssm.reshape(depth, bs, G, gw, N), state_conv)
    y_s, gla_s, ssm_s, conv_s = _trunk(cfg, x_sample.reshape(bs * ts, d), mod_s, rows_s,
                                       states_s, weights, F32)
    hs = cfg.ssd_heads
    return (y_p.reshape(bp, tp, d), y_s.reshape(bs, ts, d),
            gla_p, ssm_p.reshape(depth, bp, hs, P, N), conv_p,
            gla_s, ssm_s.reshape(depth, bs, hs, P, N), conv_s)


def kernel(x_prompt, x_sample, c_prompt, c_sample, state_gla, state_ssm, state_conv, w_ada, b_ada, w_in,
           w_gla_gate, b_gla_gate, gla_norm, w_gla_proj, conv_w, conv_b, dt_bias, A_log, d_skip, ssd_norm,
           w_ssd_proj, w_mix_out, w_ffn_in, w_ffn_out, final_norm):
    cfg = Cfg()
    assert x_prompt.shape[2] == cfg.d_model
    return _forward(cfg, x_prompt, x_sample, c_prompt, c_sample, state_gla, state_ssm, state_conv,
                    w_ada, b_ada, w_in, w_gla_gate, b_gla_gate, gla_norm, w_gla_proj, conv_w, conv_b,
                    dt_bias, A_log, d_skip, ssd_norm, w_ssd_proj, w_mix_out, w_ffn_in, w_ffn_out, final_norm)
```

```python
import dataclasses
import functools

import jax
import jax.numpy as jnp
from jax import lax
from jax.experimental import pallas as pl
from jax.experimental.pallas import tpu as pltpu

F32 = jnp.float32
BF16 = jnp.bfloat16

LANE = 128
SUBLANE = 8
VMEM_LIMIT = 56 * 1024 * 1024
EPS = 1e-6
N_ADA = 6


@dataclasses.dataclass(frozen=True)
class Cfg:
    d_model: int = 2048
    depth: int = 2
    gla_heads: int = 4
    gla_rank: int = 16
    gla_tau: float = 16.0
    gla_chunk: int = 64
    ssd_head_dim: int = 64
    ssd_groups: int = 8
    ssd_state: int = 128
    ssd_conv: int = 4
    ssd_chunk: int = 64

    @property
    def dk(self):
        return self.d_model // 2 // self.gla_heads

    @property
    def dv(self):
        return self.d_model // self.gla_heads

    @property
    def inner(self):
        return 2 * self.d_model

    @property
    def ssd_heads(self):
        return self.inner // self.ssd_head_dim

    @property
    def hpg(self):
        return self.ssd_heads // self.ssd_groups

    @property
    def gw(self):
        return self.inner // self.ssd_groups

    @property
    def conv_dim(self):
        return self.inner + 2 * self.ssd_groups * self.ssd_state

    @property
    def d_ff(self):
        return ((8 * self.d_model // 3 + 255) // 256) * 256

    @property
    def off_z(self):
        return 0

    @property
    def off_xbc(self):
        return self.inner

    @property
    def off_q(self):
        return self.off_xbc + self.conv_dim

    @property
    def off_k(self):
        return self.off_q + self.gla_heads * self.dk

    @property
    def off_v(self):
        return self.off_k + self.gla_heads * self.dk

    @property
    def off_r(self):
        return self.off_v + self.gla_heads * self.dv

    @property
    def off_gates(self):
        return self.off_r + self.gla_heads * self.dv

    @property
    def off_glr(self):
        return self.off_gates + 2 * self.d_model

    @property
    def off_dt(self):
        return self.off_glr + LANE

    def n_packed(self, tn):
        n = self.off_dt + LANE
        return -(-n // tn) * tn


def _cparams(sem):
    return pltpu.CompilerParams(dimension_semantics=sem, vmem_limit_bytes=VMEM_LIMIT)


def _sigmoid(x):
    return 1.0 / (1.0 + jnp.exp(-x))


def _silu(x):
    return x * _sigmoid(x)


def _softplus(x):
    return jnp.maximum(x, 0.0) + jnp.log1p(jnp.exp(-jnp.abs(x)))


def _log_sigmoid(x):
    return jnp.minimum(x, 0.0) - jnp.log1p(jnp.exp(-jnp.abs(x)))


def _dot(a, b):
    return jnp.dot(a, b, preferred_element_type=F32)


def _dot_nt(a, b):
    return lax.dot_general(a, b, (((1,), (1,)), ((), ())), preferred_element_type=F32)


def _dot_tn(a, b):
    return lax.dot_general(a, b, (((0,), (0,)), ((), ())), preferred_element_type=F32)


def _split(x, parts):
    out = []
    r = x
    for i in range(parts):
        p = r.astype(BF16)
        out.append(p)
        if i + 1 < parts:
            r = r - p.astype(F32)
    return out


def _rms(x):
    return x * lax.rsqrt(jnp.mean(x * x, axis=-1, keepdims=True) + EPS)


def _expand_rows(v, reps):
    if reps == 1:
        return v
    g = v.shape[0]
    r = lax.broadcasted_iota(jnp.int32, (g * reps, g), 0)
    c = lax.broadcasted_iota(jnp.int32, (g * reps, g), 1)
    lo = c * reps
    e = jnp.where((r >= lo) & (r < lo + reps), 1.0, 0.0).astype(BF16)
    acc = None
    for p in _split(v, 3):
        d = _dot(e, p)
        acc = d if acc is None else acc + d
    return acc


def _tri(n, strict=False, upper=False):
    r = lax.broadcasted_iota(jnp.int32, (n, n), 0)
    c = lax.broadcasted_iota(jnp.int32, (n, n), 1)
    if upper:
        r, c = c, r
    return (r > c) if strict else (r >= c)


def _ada_kernel(c_ref, w_ref, b_ref, o_ref):
    a = _silu(c_ref[...]).astype(BF16)
    o_ref[...] = _dot(a, w_ref[...].astype(BF16)) + b_ref[...]


def _prenorm_kernel(x_ref, sc_ref, sh_ref, o_ref, *, reps):
    n = _rms(x_ref[...])
    sc = _expand_rows(sc_ref[...], reps)
    sh = _expand_rows(sh_ref[...], reps)
    o_ref[...] = (n * (1.0 + sc) + sh).astype(o_ref.dtype)


def _final_norm_kernel(x_ref, g_ref, o_ref):
    o_ref[...] = _rms(x_ref[...]) * g_ref[...]


def _mm_kernel(a_ref, w_ref, o_ref):
    o_ref[...] = _dot(a_ref[...], w_ref[...]).astype(o_ref.dtype)


def _merge_kernel(oa_ref, yb_ref, wa_ref, wb_ref, ga_ref, gb_ref, o_ref):
    ya = _dot(oa_ref[...].astype(BF16), wa_ref[...])
    yb = _dot(yb_ref[...].astype(BF16), wb_ref[...])
    o_ref[...] = (_sigmoid(ga_ref[...]) * ya + _sigmoid(gb_ref[...]) * yb).astype(o_ref.dtype)


def _resid_kernel(a_ref, w_ref, x_ref, g_ref, o_ref, *, reps):
    y = _dot(a_ref[...], w_ref[...])
    o_ref[...] = x_ref[...] + _expand_rows(g_ref[...], reps) * y


def _ffn_in_kernel(h_ref, wg_ref, wu_ref, o_ref):
    h = h_ref[...]
    o_ref[...] = (_silu(_dot(h, wg_ref[...])) * _dot(h, wu_ref[...])).astype(o_ref.dtype)


def _gla_kernel(*refs, L, nc, nt, H, zero_init, has_prev, scale, inv_tau):
    q_ref, k_ref, v_ref, glr_ref, r_ref, wg_ref, bg_ref, gn_ref = refs[:8]
    i = 8
    s0_ref = None
    if not zero_init:
        s0_ref = refs[i]
        i += 1
    if has_prev:
        i += 1
    o_ref, s_out_ref, s_scr = refs[i:i + 3]
    t = pl.program_id(1)
    _, dk, dv = s_scr.shape

    @pl.when(t == 0)
    def _init():
        if zero_init:
            s_scr[...] = jnp.zeros_like(s_scr)
        else:
            s_scr[...] = s0_ref[...]

    ga = _dot(glr_ref[...].astype(BF16), wg_ref[...]) + bg_ref[...]
    log_a = _log_sigmoid(ga) * inv_tau
    tri = jnp.where(_tri(L), 1.0, 0.0).astype(BF16)
    mask = _tri(L)
    ones = jnp.ones((L, LANE), BF16)
    for c in range(nc):
        sl = slice(c * L, (c + 1) * L)
        for h in range(H):
            ck = slice(h * dk, (h + 1) * dk)
            cv = slice(h * dv, (h + 1) * dv)
            g_parts = _split(log_a[sl, ck], 2)
            b = _dot(tri, g_parts[0]) + _dot(tri, g_parts[1])
            tot = _dot_tn(g_parts[0], ones) + _dot_tn(g_parts[1], ones)
            q = q_ref[sl, ck] * scale
            k = k_ref[sl, ck]
            v = v_ref[sl, cv].astype(BF16)
            q_in = (q * jnp.exp(b)).astype(BF16)
            k_in = (k * jnp.exp(-b)).astype(BF16)
            k_out = (k * jnp.exp(b[L - 1:L, :] - b)).astype(BF16)
            att = jnp.where(mask, _dot_nt(q_in, k_in), 0.0)
            s = s_scr[h]
            o = _dot(q_in, s.astype(BF16)) + _dot(att.astype(BF16), v)
            decay = jnp.exp(tot)
            s_scr[h] = s * jnp.concatenate([decay] * (dv // LANE), axis=1) + _dot_tn(k_out, v)
            on = _rms(o) * gn_ref[...]
            o_ref[sl, cv] = (on * _silu(r_ref[sl, cv])).astype(o_ref.dtype)

    @pl.when(t == nt - 1)
    def _fin():
        s_out_ref[...] = s_scr[...]


def _ssd_kernel(*refs, L, nc, nt, G, hpg, P, N, zero_init, has_prev, has_exl):
    (z_ref, x_ref, b_ref, c_ref, dt_ref, cw_ref, cb_ref, dtb_ref, alog_ref, dsk_ref, nrm_ref, exp_ref) = refs[:12]
    i = 12
    exl_ref = exp_ref
    if has_exl:
        exl_ref = refs[i]
        i += 1
    cs_ref = h0_ref = None
    if not zero_init:
        cs_ref, h0_ref = refs[i:i + 2]
        i += 2
    if has_prev:
        i += 1
    y_ref, h_out_ref, u, h_scr, xbd = refs[i:i + 5]
    t = pl.program_id(1)
    Tb = L * nc
    gw = hpg * P
    inner = G * gw
    wl = hpg * L
    nconv = cw_ref.shape[0]
    halo = SUBLANE

    @pl.when(t == 0)
    def _init():
        u[0:halo, :] = jnp.zeros((halo, u.shape[1]), F32)
        if zero_init:
            h_scr[...] = jnp.zeros_like(h_scr)
        else:
            u[halo - (nconv - 1):halo, :] = cs_ref[...]
            h_scr[...] = h0_ref[...].reshape(inner, N)
        xbd[...] = jnp.zeros_like(xbd)

    u[halo:halo + Tb, 0:inner] = x_ref[...]
    u[halo:halo + Tb, inner:inner + G * N] = b_ref[...]
    u[halo:halo + Tb, inner + G * N:inner + 2 * G * N] = c_ref[...]

    def conv(lo, width):
        acc = cb_ref[:, lo:lo + width]
        for j in range(nconv):
            acc = acc + u[pl.ds(halo - (nconv - 1) + j, Tb), lo:lo + width] * cw_ref[j:j + 1, lo:lo + width]
        return _silu(acc)

    dtp = _softplus(dt_ref[...] + dtb_ref[...])
    a_h = dtp * (-jnp.exp(alog_ref[...]))
    dtp_parts = _split(dtp, 2)
    a_parts = _split(a_h, 2)

    def expand(parts, ex):
        return _dot(parts[0], ex) + _dot(parts[1], ex)

    tri = jnp.where(_tri(L), 1.0, 0.0).astype(BF16)
    ones = jnp.ones((L, LANE), BF16)
    row_l = lax.broadcasted_iota(jnp.int32, (L, wl), 0)
    s_l = lax.broadcasted_iota(jnp.int32, (L, wl), 1) % L
    later = row_l > s_l
    causal = row_l >= s_l
    early_cast = L % 16 == 0
    lane_in_piece = lax.broadcasted_iota(jnp.int32, (L, LANE), 1)

    for g in range(G):
        gc = slice(g * gw, (g + 1) * gw)
        xc = conv(g * gw, gw)
        bc = conv(inner + g * N, N)
        cc = conv(inner + G * N + g * N, N)
        ex_g = exp_ref[:, gc]
        dt_exp = expand(dtp_parts, ex_g)
        a_exp_p = expand(a_parts, ex_g)
        a_exp_l = expand(a_parts, exl_ref[:, g * wl:(g + 1) * wl]) if has_exl else a_exp_p
        for c in range(nc):
            sl = slice(c * L, (c + 1) * L)
            x = xc[sl]
            b = bc[sl]
            cm = cc[sl].astype(BF16)
            ap = _split(a_exp_p[sl], 2)
            al = _split(jnp.where(later, a_exp_l[sl], 0.0), 2)
            ccol = _dot(tri, ap[0]) + _dot(tri, ap[1])
            seg = _dot(tri, al[0]) + _dot(tri, al[1])
            decay = jnp.where(causal, jnp.exp(seg), 0.0)
            if early_cast:
                b_t = jnp.concatenate([b.astype(BF16)] * hpg, axis=0)
            else:
                b_t = jnp.concatenate([b] * hpg, axis=0).astype(BF16)
            w_cat = (_dot_nt(cm, b_t) * decay).astype(BF16)
            xt = x * dt_exp[sl]
            xt_c = xt.astype(xbd.dtype)
            slab = g * nc + c
            for hh in range(hpg):
                lo = (hh * P) // LANE * LANE
                piece = xt_c[:, lo:lo + LANE]
                own = (lane_in_piece >= hh * P - lo) & (lane_in_piece < (hh + 1) * P - lo)
                xbd[slab, hh * L:(hh + 1) * L, lo:lo + LANE] = jnp.where(own, piece, jnp.zeros_like(piece))
            h = h_scr[gc, :]
            y = _dot(w_cat, xbd[slab].astype(BF16)) + jnp.exp(ccol) * _dot_nt(cm, h.astype(BF16))
            y = y + dsk_ref[:, gc] * x
            x_dl = (xt * jnp.exp(ccol[L - 1:L, :] - ccol)).astype(BF16)
            tot = _dot_tn(ap[0], ones) + _dot_tn(ap[1], ones)
            h_scr[gc, :] = h * jnp.concatenate([jnp.exp(tot)] * (N // LANE), axis=1) + _dot_tn(x_dl, b.astype(BF16))
            yg = y * _silu(z_ref[sl, gc])
            y_ref[sl, gc] = (_rms(yg) * nrm_ref[:, gc]).astype(y_ref.dtype)

    u[0:halo, :] = u[Tb:Tb + halo, :]

    @pl.when(t == nt - 1)
    def _fin():
        h_out_ref[...] = h_scr[...].reshape(h_out_ref.shape)


def _ada(c_all, w_ada, b_ada, tn=512):
    depth, d, n = w_ada.shape
    m = c_all.shape[0]
    return pl.pallas_call(
        _ada_kernel,
        grid=(depth, n // tn),
        in_specs=[
            pl.BlockSpec((m, d), lambda l, j: (0, 0)),
            pl.BlockSpec((None, d, tn), lambda l, j: (l, 0, j)),
            pl.BlockSpec((None, 1, tn), lambda l, j: (l, 0, j)),
        ],
        out_specs=pl.BlockSpec((None, m, tn), lambda l, j: (l, 0, j)),
        out_shape=jax.ShapeDtypeStruct((depth, m, n), F32),
        compiler_params=_cparams(("parallel", "parallel")),
        name="ada",
    )(c_all, w_ada, b_ada.reshape(depth, 1, n))


class _Rows:
    def __init__(self, n_seq, seq_len, long_seq, tm_max=1024):
        self.n_seq, self.seq_len, self.long_seq = n_seq, seq_len, long_seq
        self.m = n_seq * seq_len
        if long_seq:
            self.tm = min(tm_max, seq_len)
            assert seq_len % self.tm == 0
            self.reps = 1
            self.mod_rows = 1
        else:
            self.tm = min(self.m, tm_max)
            assert self.m % self.tm == 0 and self.tm % seq_len == 0
            self.reps = seq_len
            self.mod_rows = self.tm // seq_len
        self.n_tiles = self.m // self.tm

    def lhs_spec(self, width):
        if self.n_tiles == 1:
            return pl.BlockSpec((self.tm, width), lambda i, j: (i, 0), pipeline_mode=pl.Buffered(1))
        return pl.BlockSpec((self.tm, width), lambda i, j: (i, 0))

    def mod_array(self, mod):
        depth, _, n = mod.shape
        if self.reps == 1:
            return mod.reshape(depth, self.n_seq, 1, n)
        return mod

    def mod_spec(self, l, tn, col_block):
        if self.reps == 1:
            per = self.seq_len // self.tm
            return pl.BlockSpec((None, None, 1, tn), lambda i, j: (l, i // per, 0, col_block(j)))
        return pl.BlockSpec((None, self.mod_rows, tn), lambda i, j: (l, i, col_block(j)))


def _prenorm(x, mod_arr, rows, l, v_scale, v_shift):
    m, d = x.shape
    return pl.pallas_call(
        functools.partial(_prenorm_kernel, reps=rows.reps),
        grid=(rows.n_tiles, 1),
        in_specs=[
            pl.BlockSpec((rows.tm, d), lambda i, j: (i, 0)),
            rows.mod_spec(l, d, lambda j: v_scale),
            rows.mod_spec(l, d, lambda j: v_shift),
        ],
        out_specs=pl.BlockSpec((rows.tm, d), lambda i, j: (i, 0)),
        out_shape=jax.ShapeDtypeStruct((m, d), BF16),
        compiler_params=_cparams(("parallel", "arbitrary")),
        name="prenorm",
    )(x, mod_arr, mod_arr)


def _final_norm(x, gain, rows):
    m, d = x.shape
    return pl.pallas_call(
        _final_norm_kernel,
        grid=(rows.n_tiles,),
        in_specs=[pl.BlockSpec((rows.tm, d), lambda i: (i, 0)), pl.BlockSpec((1, d), lambda i: (0, 0))],
        out_specs=pl.BlockSpec((rows.tm, d), lambda i: (i, 0)),
        out_shape=jax.ShapeDtypeStruct((m, d), F32),
        compiler_params=_cparams(("parallel",)),
        name="final_norm",
    )(x, gain.reshape(1, d))


def _in_proj(h, w, l, rows, tn):
    m, d = h.shape
    n = w.shape[2]
    return pl.pallas_call(
        _mm_kernel,
        grid=(rows.n_tiles, n // tn),
        in_specs=[
            rows.lhs_spec(d),
            pl.BlockSpec((None, d, tn), lambda i, j: (l, 0, j)),
        ],
        out_specs=pl.BlockSpec((rows.tm, tn), lambda i, j: (i, j)),
        out_shape=jax.ShapeDtypeStruct((m, n), F32),
        compiler_params=_cparams(("parallel", "arbitrary")),
        name="in_proj",
    )(h, w)


def _merge(cfg, oa, yb, proj, wa, wb, l, rows, tn=256):
    m = oa.shape[0]
    d = cfg.d_model
    ga0 = cfg.off_gates // tn
    gb0 = (cfg.off_gates + d) // tn
    return pl.pallas_call(
        _merge_kernel,
        grid=(rows.n_tiles, d // tn),
        in_specs=[
            rows.lhs_spec(oa.shape[1]),
            rows.lhs_spec(yb.shape[1]),
            pl.BlockSpec((None, wa.shape[1], tn), lambda i, j: (l, 0, j)),
            pl.BlockSpec((None, wb.shape[1], tn), lambda i, j: (l, 0, j)),
            pl.BlockSpec((rows.tm, tn), lambda i, j: (i, ga0 + j)),
            pl.BlockSpec((rows.tm, tn), lambda i, j: (i, gb0 + j)),
        ],
        out_specs=pl.BlockSpec((rows.tm, tn), lambda i, j: (i, j)),
        out_shape=jax.ShapeDtypeStruct((m, d), BF16),
        compiler_params=_cparams(("parallel", "arbitrary")),
        name="merge",
    )(oa, yb, wa, wb, proj, proj)


def _resid(a, w, x, mod_arr, rows, l, v_gate, tn=256, name="resid"):
    m, kdim = a.shape
    d = x.shape[1]
    per = d // tn
    return pl.pallas_call(
        functools.partial(_resid_kernel, reps=rows.reps),
        grid=(rows.n_tiles, d // tn),
        in_specs=[
            rows.lhs_spec(kdim),
            pl.BlockSpec((None, kdim, tn), lambda i, j: (l, 0, j)),
            pl.BlockSpec((rows.tm, tn), lambda i, j: (i, j)),
            rows.mod_spec(l, tn, lambda j: v_gate * per + j),
        ],
        out_specs=pl.BlockSpec((rows.tm, tn), lambda i, j: (i, j)),
        out_shape=jax.ShapeDtypeStruct((m, d), F32),
        compiler_params=_cparams(("parallel", "arbitrary")),
        name=name,
    )(a, w, x, mod_arr)


def _ffn_in(h, w, l, rows, d_ff, tn=512):
    m, d = h.shape
    nb = d_ff // tn
    return pl.pallas_call(
        _ffn_in_kernel,
        grid=(rows.n_tiles, nb),
        in_specs=[
            rows.lhs_spec(d),
            pl.BlockSpec((None, d, tn), lambda i, j: (l, 0, j)),
            pl.BlockSpec((None, d, tn), lambda i, j: (l, 0, nb + j)),
        ],
        out_specs=pl.BlockSpec((rows.tm, tn), lambda i, j: (i, j)),
        out_shape=jax.ShapeDtypeStruct((m, d_ff), BF16),
        compiler_params=_cparams(("parallel", "arbitrary")),
        name="ffn_in",
    )(h, w, w)


def _scan_blocks(rows, chunk, max_chunks):
    L = min(chunk, rows.seq_len)
    assert rows.seq_len % L == 0
    n_chunks = rows.seq_len // L
    nc = min(max_chunks, n_chunks)
    assert n_chunks % nc == 0
    return L, nc, n_chunks // nc


def _gla(cfg, proj, l, rows, wg, bg, gn, state, prev, out_dtype):
    H, dk, dv = cfg.gla_heads, cfg.dk, cfg.dv
    hk, hv = H * dk, H * dv
    B = rows.n_seq
    L, nc, nt = _scan_blocks(rows, cfg.gla_chunk, 4)
    Tb = L * nc
    assert cfg.off_q % hk == 0 and cfg.off_v % hv == 0
    rb = lambda b, t: b * nt + t
    in_specs = [
        pl.BlockSpec((Tb, hk), lambda b, t: (rb(b, t), cfg.off_q // hk)),
        pl.BlockSpec((Tb, hk), lambda b, t: (rb(b, t), cfg.off_k // hk)),
        pl.BlockSpec((Tb, hv), lambda b, t: (rb(b, t), cfg.off_v // hv)),
        pl.BlockSpec((Tb, LANE), lambda b, t: (rb(b, t), cfg.off_glr // LANE)),
        pl.BlockSpec((Tb, hv), lambda b, t: (rb(b, t), cfg.off_r // hv)),
        pl.BlockSpec((None, LANE, hk), lambda b, t: (l, 0, 0)),
        pl.BlockSpec((None, 1, hk), lambda b, t: (l, 0, 0)),
        pl.BlockSpec((None, 1, dv), lambda b, t: (l, 0, 0)),
    ]
    args = [proj, proj, proj, proj, proj, wg, bg, gn]
    state_spec = pl.BlockSpec((None, None, H, dk, dv), lambda b, t: (l, b, 0, 0, 0))
    if state is not None:
        in_specs.append(state_spec)
        args.append(state)
    aliases = {}
    if prev is not None:
        in_specs.append(pl.BlockSpec(memory_space=pl.ANY))
        aliases = {len(args): 1}
        args.append(prev)
    kern = functools.partial(_gla_kernel, L=L, nc=nc, nt=nt, H=H, zero_init=state is None,
                             has_prev=prev is not None, scale=dk ** -0.5, inv_tau=1.0 / cfg.gla_tau)
    return pl.pallas_call(
        kern,
        grid=(B, nt),
        in_specs=in_specs,
        out_specs=[pl.BlockSpec((Tb, hv), lambda b, t: (rb(b, t), 0)), state_spec],
        out_shape=[
            jax.ShapeDtypeStruct((rows.m, hv), out_dtype),
            jax.ShapeDtypeStruct((cfg.depth, B, H, dk, dv), F32),
        ],
        scratch_shapes=[pltpu.VMEM((H, dk, dv), F32)],
        input_output_aliases=aliases,
        compiler_params=_cparams(("parallel", "arbitrary")),
        name="gla_scan",
    )(*args)


def _head_expander(n_heads, per):
    col_head = jnp.arange(n_heads * per, dtype=jnp.int32) // per
    return (col_head[None, :] == jnp.arange(LANE, dtype=jnp.int32)[:, None]).astype(BF16)


def _ssd(cfg, proj, l, rows, conv_w, conv_b, dtb, alog, dsk, nrm, conv_state, state, prev, out_dtype):
    G, gw, N, P, hpg = cfg.ssd_groups, cfg.gw, cfg.ssd_state, cfg.ssd_head_dim, cfg.hpg
    inner, heads, cdim = cfg.inner, cfg.ssd_heads, cfg.conv_dim
    B = rows.n_seq
    K = cfg.ssd_conv
    L, nc, nt = _scan_blocks(rows, cfg.ssd_chunk, 2)
    Tb = L * nc
    wl = hpg * L
    gn = G * N
    assert cfg.off_z % inner == 0 and cfg.off_xbc % inner == 0 and (cfg.off_xbc + inner) % gn == 0
    rb = lambda b, t: b * nt + t
    in_specs = [
        pl.BlockSpec((Tb, inner), lambda b, t: (rb(b, t), cfg.off_z // inner)),
        pl.BlockSpec((Tb, inner), lambda b, t: (rb(b, t), cfg.off_xbc // inner)),
        pl.BlockSpec((Tb, gn), lambda b, t: (rb(b, t), (cfg.off_xbc + inner) // gn)),
        pl.BlockSpec((Tb, gn), lambda b, t: (rb(b, t), (cfg.off_xbc + inner) // gn + 1)),
        pl.BlockSpec((Tb, LANE), lambda b, t: (rb(b, t), cfg.off_dt // LANE)),
        pl.BlockSpec((None, K, cdim), lambda b, t: (l, 0, 0)),
        pl.BlockSpec((None, 1, cdim), lambda b, t: (l, 0, 0)),
        pl.BlockSpec((None, 1, LANE), lambda b, t: (l, 0, 0)),
        pl.BlockSpec((None, 1, LANE), lambda b, t: (l, 0, 0)),
        pl.BlockSpec((None, 1, inner), lambda b, t: (l, 0, 0)),
        pl.BlockSpec((None, 1, inner), lambda b, t: (l, 0, 0)),
        pl.BlockSpec((LANE, inner), lambda b, t: (0, 0)),
    ]
    args = [proj, proj, proj, proj, proj, conv_w, conv_b, dtb, alog, dsk, nrm, _head_expander(heads, P)]
    has_exl = L != P
    if has_exl:
        in_specs.append(pl.BlockSpec((LANE, heads * L), lambda b, t: (0, 0)))
        args.append(_head_expander(heads, L))
    state_spec = pl.BlockSpec((None, None, heads, P, N), lambda b, t: (l, b, 0, 0, 0))
    if state is not None:
        in_specs += [pl.BlockSpec((None, None, K - 1, cdim), lambda b, t: (l, b, 0, 0)), state_spec]
        args += [conv_state, state]
    aliases = {}
    if prev is not None:
        in_specs.append(pl.BlockSpec(memory_space=pl.ANY))
        aliases = {len(args): 1}
        args.append(prev)
    kern = functools.partial(_ssd_kernel, L=L, nc=nc, nt=nt, G=G, hpg=hpg, P=P, N=N, zero_init=state is None,
                             has_prev=prev is not None, has_exl=has_exl)
    xbd_dtype = BF16 if L % 16 == 0 else F32
    return pl.pallas_call(
        kern,
        grid=(B, nt),
        in_specs=in_specs,
        out_specs=[pl.BlockSpec((Tb, inner), lambda b, t: (rb(b, t), 0)), state_spec],
        out_shape=[
            jax.ShapeDtypeStruct((rows.m, inner), out_dtype),
            jax.ShapeDtypeStruct((cfg.depth, B, heads, P, N), F32),
        ],
        scratch_shapes=[
            pltpu.VMEM((Tb + SUBLANE, cdim), F32),
            pltpu.VMEM((inner, N), F32),
            pltpu.VMEM((G * nc, wl, gw), xbd_dtype),
        ],
        input_output_aliases=aliases,
        compiler_params=_cparams(("parallel", "arbitrary")),
        name="ssd_scan",
    )(*args)


def _pack_w_in(cfg, w_in, tn):
    H, dk, dv = cfg.gla_heads, cfg.dk, cfg.dv
    sizes = (H * dk, H * dk, H * dv, H * dv, cfg.gla_rank, cfg.inner, cfg.conv_dim, cfg.ssd_heads, 2 * cfg.d_model)
    offs = [0]
    for s in sizes:
        offs.append(offs[-1] + s)
    seg = lambda i: w_in[:, :, offs[i]:offs[i + 1]]
    padto = lambda a, n: jnp.pad(a, ((0, 0), (0, 0), (0, n - a.shape[2])))
    n_p = cfg.n_packed(tn)
    parts = [seg(5), seg(6), seg(0), seg(1), seg(2), seg(3), seg(8), padto(seg(4), LANE), padto(seg(7), LANE)]
    packed = jnp.concatenate(parts, axis=2)
    return padto(packed, n_p).astype(BF16)


def _trunk(cfg, x, mod, rows, states, weights, scan_dtype):
    (w_in_p, wg, bg, gn, w_gla_proj, conv_w, conv_b, dtb, alog, dsk, nrm, w_ssd_proj, w_mix_out,
     w_ffn_in, w_ffn_out, final_norm, tn_in) = weights
    s_gla, s_ssm, s_conv = states
    mod_arr = rows.mod_array(mod)
    rows_e = _Rows(rows.n_seq, rows.seq_len, rows.long_seq, tm_max=256)
    B = rows.n_seq
    K = cfg.ssd_conv
    new_gla = new_ssm = None
    new_conv = []
    for l in range(cfg.depth):
        h = _prenorm(x, mod_arr, rows_e, l, 1, 0)
        proj = _in_proj(h, w_in_p, l, rows, tn_in)
        oa, new_gla = _gla(cfg, proj, l, rows, wg, bg, gn, s_gla, new_gla, scan_dtype)
        yb, new_ssm = _ssd(cfg, proj, l, rows, conv_w, conv_b, dtb, alog, dsk, nrm, s_conv, s_ssm, new_ssm,
                           scan_dtype)
        assert rows.seq_len >= K - 1
        tail = proj.reshape(B, rows.seq_len, proj.shape[1])[:, rows.seq_len - (K - 1):, :]
        new_conv.append(tail[:, :, cfg.off_xbc:cfg.off_xbc + cfg.conv_dim])
        merged = _merge(cfg, oa, yb, proj, w_gla_proj, w_ssd_proj, l, rows)
        x = _resid(merged, w_mix_out, x, mod_arr, rows, l, 2, name="mix_out")
        h = _prenorm(x, mod_arr, rows_e, l, 4, 3)
        act = _ffn_in(h, w_ffn_in, l, rows, cfg.d_ff)
        x = _resid(act, w_ffn_out, x, mod_arr, rows, l, 5, name="ffn_out")
    y = _final_norm(x, final_norm, rows_e)
    return y, new_gla, new_ssm, jnp.stack(new_conv)


def _forward(cfg, x_prompt, x_sample, c_prompt, c_sample, state_gla, state_ssm, state_conv,
             w_ada, b_ada, w_in, w_gla_gate, b_gla_gate, gla_norm, w_gla_proj, conv_w, conv_b,
             dt_bias, A_log, d_skip, ssd_norm, w_ssd_proj, w_mix_out, w_ffn_in, w_ffn_out, final_norm):
    depth, d = cfg.depth, cfg.d_model
    bp, tp, _ = x_prompt.shape
    bs, ts, _ = x_sample.shape
    H, dk, dv = cfg.gla_heads, cfg.dk, cfg.dv
    P = cfg.ssd_head_dim
    assert cfg.ssd_heads <= LANE and cfg.gla_rank <= LANE
    assert cfg.off_gates % 256 == 0

    pad = (-bp) % SUBLANE
    c_all = jnp.concatenate([c_prompt, jnp.zeros((pad, d), F32), c_sample], axis=0)
    mod = _ada(c_all, w_ada, b_ada)
    mod_p = mod[:, :bp]
    mod_s = mod[:, bp + pad:]

    tn_in = 512
    padl = lambda a: jnp.pad(a, ((0, 0), (0, LANE - a.shape[1])))
    weights = (
        _pack_w_in(cfg, w_in, tn_in),
        jnp.pad(w_gla_gate, ((0, 0), (0, LANE - cfg.gla_rank), (0, 0))).astype(BF16),
        b_gla_gate.reshape(depth, 1, H * dk),
        gla_norm.reshape(depth, 1, dv),
        w_gla_proj.astype(BF16),
        conv_w,
        conv_b.reshape(depth, 1, cfg.conv_dim),
        padl(dt_bias).reshape(depth, 1, LANE),
        padl(A_log).reshape(depth, 1, LANE),
        jnp.repeat(d_skip, P, axis=1).reshape(depth, 1, cfg.inner),
        ssd_norm.reshape(depth, 1, cfg.inner),
        w_ssd_proj.astype(BF16),
        w_mix_out.astype(BF16),
        w_ffn_in.astype(BF16),
        w_ffn_out.astype(BF16),
        final_norm,
        tn_in,
    )
    rows_p = _Rows(bp, tp, True)
    rows_s = _Rows(bs, ts, False)
    y_p, gla_p, ssm_p, conv_p = _trunk(cfg, x_prompt.reshape(bp * tp, d), mod_p, rows_p,
                                       (None, None, None), weights, BF16)
    states_s = (state_gla, state_ssm, state_conv)
    y_s, gla_s, ssm_s, conv_s = _trunk(cfg, x_sample.reshape(bs * ts, d), mod_s, rows_s,
                                       states_s, weights, F32)
    return (y_p.reshape(bp, tp, d), y_s.reshape(bs, ts, d), gla_p, ssm_p, conv_p, gla_s, ssm_s, conv_s)


def kernel(x_prompt, x_sample, c_prompt, c_sample, state_gla, state_ssm, state_conv, w_ada, b_ada, w_in,
           w_gla_gate, b_gla_gate, gla_norm, w_gla_proj, conv_w, conv_b, dt_bias, A_log, d_skip, ssd_norm,
           w_ssd_proj, w_mix_out, w_ffn_in, w_ffn_out, final_norm):
    cfg = Cfg()
    assert x_prompt.shape[2] == cfg.d_model
    return _forward(cfg, x_prompt, x_sample, c_prompt, c_sample, state_gla, state_ssm, state_conv,
                    w_ada, b_ada, w_in, w_gla_gate, b_gla_gate, gla_norm, w_gla_proj, conv_w, conv_b,
                    dt_bias, A_log, d_skip, ssd_norm, w_ssd_proj, w_mix_out, w_ffn_in, w_ffn_out, final_norm)
```

```python
import dataclasses
import functools

import jax
import jax.numpy as jnp
from jax import lax
from jax.experimental import pallas as pl
from jax.experimental.pallas import tpu as pltpu

F32 = jnp.float32
BF16 = jnp.bfloat16

LANE = 128
SUBLANE = 8
VMEM_LIMIT = 56 * 1024 * 1024
EPS = 1e-6
N_ADA = 6


@dataclasses.dataclass(frozen=True)
class Cfg:
    d_model: int = 2048
    depth: int = 2
    gla_heads: int = 4
    gla_rank: int = 16
    gla_tau: float = 16.0
    gla_chunk: int = 64
    ssd_head_dim: int = 64
    ssd_groups: int = 8
    ssd_state: int = 128
    ssd_conv: int = 4
    ssd_chunk: int = 64

    @property
    def dk(self):
        return self.d_model // 2 // self.gla_heads

    @property
    def dv(self):
        return self.d_model // self.gla_heads

    @property
    def inner(self):
        return 2 * self.d_model

    @property
    def ssd_heads(self):
        return self.inner // self.ssd_head_dim

    @property
    def hpg(self):
        return self.ssd_heads // self.ssd_groups

    @property
    def gw(self):
        return self.inner // self.ssd_groups

    @property
    def conv_dim(self):
        return self.inner + 2 * self.ssd_groups * self.ssd_state

    @property
    def d_ff(self):
        return ((8 * self.d_model // 3 + 255) // 256) * 256

    @property
    def off_z(self):
        return 0

    @property
    def off_xbc(self):
        return self.inner

    @property
    def off_q(self):
        return self.off_xbc + self.conv_dim

    @property
    def off_k(self):
        return self.off_q + self.gla_heads * self.dk

    @property
    def off_v(self):
        return self.off_k + self.gla_heads * self.dk

    @property
    def off_r(self):
        return self.off_v + self.gla_heads * self.dv

    @property
    def off_gates(self):
        return self.off_r + self.gla_heads * self.dv

    @property
    def off_glr(self):
        return self.off_gates + 2 * self.d_model

    @property
    def off_dt(self):
        return self.off_glr + LANE

    def n_packed(self, tn):
        n = self.off_dt + LANE
        return -(-n // tn) * tn


def _cparams(sem):
    return pltpu.CompilerParams(dimension_semantics=sem, vmem_limit_bytes=VMEM_LIMIT)


def _sigmoid(x):
    return 1.0 / (1.0 + jnp.exp(-x))


def _silu(x):
    return x * _sigmoid(x)


def _softplus(x):
    return jnp.maximum(x, 0.0) + jnp.log1p(jnp.exp(-jnp.abs(x)))


def _log_sigmoid(x):
    return jnp.minimum(x, 0.0) - jnp.log1p(jnp.exp(-jnp.abs(x)))


def _dot(a, b):
    return jnp.dot(a, b, preferred_element_type=F32)


def _dot_nt(a, b):
    return lax.dot_general(a, b, (((1,), (1,)), ((), ())), preferred_element_type=F32)


def _dot_tn(a, b):
    return lax.dot_general(a, b, (((0,), (0,)), ((), ())), preferred_element_type=F32)


def _split(x, parts):
    out = []
    r = x
    for i in range(parts):
        p = r.astype(BF16)
        out.append(p)
        if i + 1 < parts:
            r = r - p.astype(F32)
    return out


def _rms(x):
    return x * lax.rsqrt(jnp.mean(x * x, axis=-1, keepdims=True) + EPS)


def _expand_rows(v, reps):
    if reps == 1:
        return v
    g = v.shape[0]
    r = lax.broadcasted_iota(jnp.int32, (g * reps, g), 0)
    c = lax.broadcasted_iota(jnp.int32, (g * reps, g), 1)
    lo = c * reps
    e = jnp.where((r >= lo) & (r < lo + reps), 1.0, 0.0).astype(BF16)
    acc = None
    for p in _split(v, 3):
        d = _dot(e, p)
        acc = d if acc is None else acc + d
    return acc


def _tri(n, strict=False, upper=False):
    r = lax.broadcasted_iota(jnp.int32, (n, n), 0)
    c = lax.broadcasted_iota(jnp.int32, (n, n), 1)
    if upper:
        r, c = c, r
    return (r > c) if strict else (r >= c)


def _ada_kernel(c_ref, w_ref, b_ref, o_ref):
    a = _silu(c_ref[...]).astype(BF16)
    o_ref[...] = _dot(a, w_ref[...].astype(BF16)) + b_ref[...]


def _prenorm_kernel(x_ref, sc_ref, sh_ref, o_ref, *, reps):
    n = _rms(x_ref[...])
    sc = _expand_rows(sc_ref[...], reps)
    sh = _expand_rows(sh_ref[...], reps)
    o_ref[...] = (n * (1.0 + sc) + sh).astype(o_ref.dtype)


def _final_norm_kernel(x_ref, g_ref, o_ref):
    o_ref[...] = _rms(x_ref[...]) * g_ref[...]


def _mm_kernel(a_ref, w_ref, o_ref):
    o_ref[...] = _dot(a_ref[...], w_ref[...]).astype(o_ref.dtype)


def _merge_kernel(oa_ref, yb_ref, wa_ref, wb_ref, ga_ref, gb_ref, o_ref):
    ya = _dot(oa_ref[...].astype(BF16), wa_ref[...])
    yb = _dot(yb_ref[...].astype(BF16), wb_ref[...])
    o_ref[...] = (_sigmoid(ga_ref[...]) * ya + _sigmoid(gb_ref[...]) * yb).astype(o_ref.dtype)


def _resid_kernel(a_ref, w_ref, x_ref, g_ref, o_ref, *, reps):
    y = _dot(a_ref[...], w_ref[...])
    o_ref[...] = x_ref[...] + _expand_rows(g_ref[...], reps) * y


def _ffn_in_kernel(h_ref, wg_ref, wu_ref, o_ref):
    h = h_ref[...]
    o_ref[...] = (_silu(_dot(h, wg_ref[...])) * _dot(h, wu_ref[...])).astype(o_ref.dtype)


def _gla_kernel(*refs, L, nc, nt, H, zero_init, has_prev, scale, inv_tau):
    q_ref, k_ref, v_ref, glr_ref, r_ref, wg_ref, bg_ref, gn_ref = refs[:8]
    i = 8
    s0_ref = None
    if not zero_init:
        s0_ref = refs[i]
        i += 1
    if has_prev:
        i += 1
    o_ref, s_out_ref, s_scr = refs[i:i + 3]
    t = pl.program_id(1)
    _, dk, dv = s_scr.shape

    @pl.when(t == 0)
    def _init():
        if zero_init:
            s_scr[...] = jnp.zeros_like(s_scr)
        else:
            s_scr[...] = s0_ref[...]

    ga = _dot(glr_ref[...].astype(BF16), wg_ref[...]) + bg_ref[...]
    log_a = _log_sigmoid(ga) * inv_tau
    tri = jnp.where(_tri(L), 1.0, 0.0).astype(BF16)
    mask = _tri(L)
    ones = jnp.ones((L, LANE), BF16)
    heads = range(H)
    ck = [slice(h * dk, (h + 1) * dk) for h in heads]
    cv = [slice(h * dv, (h + 1) * dv) for h in heads]
    for c in range(nc):
        sl = slice(c * L, (c + 1) * L)
        g_parts = _split(log_a[sl, :], 2)
        b = _dot(tri, g_parts[0]) + _dot(tri, g_parts[1])
        tot = [_dot_tn(g_parts[0][:, ck[h]], ones) + _dot_tn(g_parts[1][:, ck[h]], ones) for h in heads]
        eb = jnp.exp(b)
        q_in = (q_ref[sl, :] * scale * eb).astype(BF16)
        k = k_ref[sl, :]
        k_in = (k * jnp.exp(-b)).astype(BF16)
        k_out = (k * jnp.exp(b[L - 1:L, :] - b)).astype(BF16)
        v = [v_ref[sl, cv[h]].astype(BF16) for h in heads]
        att = [jnp.where(mask, _dot_nt(q_in[:, ck[h]], k_in[:, ck[h]]), 0.0).astype(BF16) for h in heads]
        s = [s_scr[h] for h in heads]
        o = [_dot(q_in[:, ck[h]], s[h].astype(BF16)) + _dot(att[h], v[h]) for h in heads]
        upd = [_dot_tn(k_out[:, ck[h]], v[h]) for h in heads]
        for h in heads:
            decay = jnp.exp(tot[h])
            s_scr[h] = s[h] * jnp.concatenate([decay] * (dv // LANE), axis=1) + upd[h]
            on = _rms(o[h]) * gn_ref[...]
            o_ref[sl, cv[h]] = (on * _silu(r_ref[sl, cv[h]])).astype(o_ref.dtype)

    @pl.when(t == nt - 1)
    def _fin():
        s_out_ref[...] = s_scr[...]


def _ssd_kernel(*refs, L, nc, nt, G, hpg, P, N, zero_init, has_prev, has_exl):
    (z_ref, x_ref, b_ref, c_ref, dt_ref, cw_ref, cb_ref, dtb_ref, alog_ref, dsk_ref, nrm_ref, exp_ref) = refs[:12]
    i = 12
    exl_ref = exp_ref
    if has_exl:
        exl_ref = refs[i]
        i += 1
    cs_ref = h0_ref = None
    if not zero_init:
        cs_ref, h0_ref = refs[i:i + 2]
        i += 2
    if has_prev:
        i += 1
    y_ref, h_out_ref, u, h_scr, xbd = refs[i:i + 5]
    t = pl.program_id(1)
    Tb = L * nc
    gw = hpg * P
    inner = G * gw
    wl = hpg * L
    nconv = cw_ref.shape[0]
    halo = SUBLANE

    @pl.when(t == 0)
    def _init():
        u[0:halo, :] = jnp.zeros((halo, u.shape[1]), F32)
        if zero_init:
            h_scr[...] = jnp.zeros_like(h_scr)
        else:
            u[halo - (nconv - 1):halo, :] = cs_ref[...]
            h_scr[...] = h0_ref[...].reshape(inner, N)
        xbd[...] = jnp.zeros_like(xbd)

    u[halo:halo + Tb, 0:inner] = x_ref[...]
    u[halo:halo + Tb, inner:inner + G * N] = b_ref[...]
    u[halo:halo + Tb, inner + G * N:inner + 2 * G * N] = c_ref[...]

    def conv(r0, lo, width):
        acc = cb_ref[:, lo:lo + width]
        for j in range(nconv):
            acc = acc + u[pl.ds(r0 + halo - (nconv - 1) + j, L), lo:lo + width] * cw_ref[j:j + 1, lo:lo + width]
        return _silu(acc)

    dtp = _softplus(dt_ref[...] + dtb_ref[...])
    a_h = dtp * (-jnp.exp(alog_ref[...]))
    dtp_parts = _split(dtp, 2)
    a_parts = _split(a_h, 2)

    def expand(parts, ex):
        return _dot(parts[0], ex) + _dot(parts[1], ex)

    tri = jnp.where(_tri(L), 1.0, 0.0).astype(BF16)
    ones = jnp.ones((L, LANE), BF16)
    row_l = lax.broadcasted_iota(jnp.int32, (L, G * wl), 0)
    s_l = lax.broadcasted_iota(jnp.int32, (L, G * wl), 1) % L
    later = row_l > s_l
    causal = row_l >= s_l
    early_cast = L % 16 == 0
    lane_in_piece = lax.broadcasted_iota(jnp.int32, (L, LANE), 1)
    groups = range(G)
    gc = [slice(g * gw, (g + 1) * gw) for g in groups]

    for c in range(nc):
        sl = slice(c * L, (c + 1) * L)
        ex_all = exp_ref[...]
        dt_exp = expand([p[sl] for p in dtp_parts], ex_all)
        a_exp_p = expand([p[sl] for p in a_parts], ex_all)
        a_exp_l = expand([p[sl] for p in a_parts], exl_ref[...]) if has_exl else a_exp_p
        ap = _split(a_exp_p, 2)
        al = _split(jnp.where(later, a_exp_l, 0.0), 2)
        ccol = _dot(tri, ap[0]) + _dot(tri, ap[1])
        seg = _dot(tri, al[0]) + _dot(tri, al[1])
        decay = jnp.where(causal, jnp.exp(seg), 0.0)
        grow = jnp.exp(ccol)
        fade = jnp.exp(ccol[L - 1:L, :] - ccol)

        xs, bs, w_cats, inters = [], [], [], []
        for g in groups:
            x = conv(c * L, g * gw, gw)
            b = conv(c * L, inner + g * N, N)
            cm = conv(c * L, inner + G * N + g * N, N).astype(BF16)
            if early_cast:
                b_t = jnp.concatenate([b.astype(BF16)] * hpg, axis=0)
            else:
                b_t = jnp.concatenate([b] * hpg, axis=0).astype(BF16)
            w_cats.append((_dot_nt(cm, b_t) * decay[:, g * wl:(g + 1) * wl]).astype(BF16))
            inters.append(_dot_nt(cm, h_scr[gc[g], :].astype(BF16)))
            xs.append(x)
            bs.append(b)

        xts, intras = [], []
        for g in groups:
            xt = xs[g] * dt_exp[:, gc[g]]
            xt_c = xt.astype(xbd.dtype)
            slab = g * nc + c
            for hh in range(hpg):
                lo = (hh * P) // LANE * LANE
                piece = xt_c[:, lo:lo + LANE]
                own = (lane_in_piece >= hh * P - lo) & (lane_in_piece < (hh + 1) * P - lo)
                xbd[slab, hh * L:(hh + 1) * L, lo:lo + LANE] = jnp.where(own, piece, jnp.zeros_like(piece))
            intras.append(_dot(w_cats[g], xbd[slab].astype(BF16)))
            xts.append(xt)

        for g in groups:
            x_dl = (xts[g] * fade[:, gc[g]]).astype(BF16)
            tot = _dot_tn(ap[0][:, gc[g]], ones) + _dot_tn(ap[1][:, gc[g]], ones)
            h = h_scr[gc[g], :]
            h_scr[gc[g], :] = (h * jnp.concatenate([jnp.exp(tot)] * (N // LANE), axis=1)
                               + _dot_tn(x_dl, bs[g].astype(BF16)))

        for g in groups:
            y = intras[g] + grow[:, gc[g]] * inters[g] + dsk_ref[:, gc[g]] * xs[g]
            yg = y * _silu(z_ref[sl, gc[g]])
            y_ref[sl, gc[g]] = (_rms(yg) * nrm_ref[:, gc[g]]).astype(y_ref.dtype)

    u[0:halo, :] = u[Tb:Tb + halo, :]

    @pl.when(t == nt - 1)
    def _fin():
        h_out_ref[...] = h_scr[...].reshape(h_out_ref.shape)


def _ada(c_all, w_ada, b_ada, tn=512):
    depth, d, n = w_ada.shape
    m = c_all.shape[0]
    return pl.pallas_call(
        _ada_kernel,
        grid=(depth, n // tn),
        in_specs=[
            pl.BlockSpec((m, d), lambda l, j: (0, 0)),
            pl.BlockSpec((None, d, tn), lambda l, j: (l, 0, j)),
            pl.BlockSpec((None, 1, tn), lambda l, j: (l, 0, j)),
        ],
        out_specs=pl.BlockSpec((None, m, tn), lambda l, j: (l, 0, j)),
        out_shape=jax.ShapeDtypeStruct((depth, m, n), F32),
        compiler_params=_cparams(("parallel", "parallel")),
        name="ada",
    )(c_all, w_ada, b_ada.reshape(depth, 1, n))


class _Rows:
    def __init__(self, n_seq, seq_len, long_seq, tm_max=1024):
        self.n_seq, self.seq_len, self.long_seq = n_seq, seq_len, long_seq
        self.m = n_seq * seq_len
        if long_seq:
            self.tm = min(tm_max, seq_len)
            assert seq_len % self.tm == 0
            self.reps = 1
            self.mod_rows = 1
        else:
            self.tm = min(self.m, tm_max)
            assert self.m % self.tm == 0 and self.tm % seq_len == 0
            self.reps = seq_len
            self.mod_rows = self.tm // seq_len
        self.n_tiles = self.m // self.tm

    def lhs_spec(self, width):
        if self.n_tiles == 1:
            return pl.BlockSpec((self.tm, width), lambda i, j: (i, 0), pipeline_mode=pl.Buffered(1))
        return pl.BlockSpec((self.tm, width), lambda i, j: (i, 0))

    def mod_array(self, mod):
        depth, _, n = mod.shape
        if self.reps == 1:
            return mod.reshape(depth, self.n_seq, 1, n)
        return mod

    def mod_spec(self, l, tn, col_block):
        if self.reps == 1:
            per = self.seq_len // self.tm
            return pl.BlockSpec((None, None, 1, tn), lambda i, j: (l, i // per, 0, col_block(j)))
        return pl.BlockSpec((None, self.mod_rows, tn), lambda i, j: (l, i, col_block(j)))


def _prenorm(x, mod_arr, rows, l, v_scale, v_shift):
    m, d = x.shape
    return pl.pallas_call(
        functools.partial(_prenorm_kernel, reps=rows.reps),
        grid=(rows.n_tiles, 1),
        in_specs=[
            pl.BlockSpec((rows.tm, d), lambda i, j: (i, 0)),
            rows.mod_spec(l, d, lambda j: v_scale),
            rows.mod_spec(l, d, lambda j: v_shift),
        ],
        out_specs=pl.BlockSpec((rows.tm, d), lambda i, j: (i, 0)),
        out_shape=jax.ShapeDtypeStruct((m, d), BF16),
        compiler_params=_cparams(("parallel", "arbitrary")),
        name="prenorm",
    )(x, mod_arr, mod_arr)


def _final_norm(x, gain, rows):
    m, d = x.shape
    return pl.pallas_call(
        _final_norm_kernel,
        grid=(rows.n_tiles,),
        in_specs=[pl.BlockSpec((rows.tm, d), lambda i: (i, 0)), pl.BlockSpec((1, d), lambda i: (0, 0))],
        out_specs=pl.BlockSpec((rows.tm, d), lambda i: (i, 0)),
        out_shape=jax.ShapeDtypeStruct((m, d), F32),
        compiler_params=_cparams(("parallel",)),
        name="final_norm",
    )(x, gain.reshape(1, d))


def _in_proj(h, w, l, rows, tn):
    m, d = h.shape
    n = w.shape[2]
    return pl.pallas_call(
        _mm_kernel,
        grid=(rows.n_tiles, n // tn),
        in_specs=[
            rows.lhs_spec(d),
            pl.BlockSpec((None, d, tn), lambda i, j: (l, 0, j)),
        ],
        out_specs=pl.BlockSpec((rows.tm, tn), lambda i, j: (i, j)),
        out_shape=jax.ShapeDtypeStruct((m, n), F32),
        compiler_params=_cparams(("parallel", "arbitrary")),
        name="in_proj",
    )(h, w)


def _merge(cfg, oa, yb, proj, wa, wb, l, rows, tn=256):
    m = oa.shape[0]
    d = cfg.d_model
    ga0 = cfg.off_gates // tn
    gb0 = (cfg.off_gates + d) // tn
    return pl.pallas_call(
        _merge_kernel,
        grid=(rows.n_tiles, d // tn),
        in_specs=[
            rows.lhs_spec(oa.shape[1]),
            rows.lhs_spec(yb.shape[1]),
            pl.BlockSpec((None, wa.shape[1], tn), lambda i, j: (l, 0, j)),
            pl.BlockSpec((None, wb.shape[1], tn), lambda i, j: (l, 0, j)),
            pl.BlockSpec((rows.tm, tn), lambda i, j: (i, ga0 + j)),
            pl.BlockSpec((rows.tm, tn), lambda i, j: (i, gb0 + j)),
        ],
        out_specs=pl.BlockSpec((rows.tm, tn), lambda i, j: (i, j)),
        out_shape=jax.ShapeDtypeStruct((m, d), BF16),
        compiler_params=_cparams(("parallel", "arbitrary")),
        name="merge",
    )(oa, yb, wa, wb, proj, proj)


def _resid(a, w, x, mod_arr, rows, l, v_gate, tn=256, name="resid"):
    m, kdim = a.shape
    d = x.shape[1]
    per = d // tn
    return pl.pallas_call(
        functools.partial(_resid_kernel, reps=rows.reps),
        grid=(rows.n_tiles, d // tn),
        in_specs=[
            rows.lhs_spec(kdim),
            pl.BlockSpec((None, kdim, tn), lambda i, j: (l, 0, j)),
            pl.BlockSpec((rows.tm, tn), lambda i, j: (i, j)),
            rows.mod_spec(l, tn, lambda j: v_gate * per + j),
        ],
        out_specs=pl.BlockSpec((rows.tm, tn), lambda i, j: (i, j)),
        out_shape=jax.ShapeDtypeStruct((m, d), F32),
        compiler_params=_cparams(("parallel", "arbitrary")),
        name=name,
    )(a, w, x, mod_arr)


def _ffn_in(h, w, l, rows, d_ff, tn=512):
    m, d = h.shape
    nb = d_ff // tn
    return pl.pallas_call(
        _ffn_in_kernel,
        grid=(rows.n_tiles, nb),
        in_specs=[
            rows.lhs_spec(d),
            pl.BlockSpec((None, d, tn), lambda i, j: (l, 0, j)),
            pl.BlockSpec((None, d, tn), lambda i, j: (l, 0, nb + j)),
        ],
        out_specs=pl.BlockSpec((rows.tm, tn), lambda i, j: (i, j)),
        out_shape=jax.ShapeDtypeStruct((m, d_ff), BF16),
        compiler_params=_cparams(("parallel", "arbitrary")),
        name="ffn_in",
    )(h, w, w)


def _scan_blocks(rows, chunk, max_chunks):
    L = min(chunk, rows.seq_len)
    assert rows.seq_len % L == 0
    n_chunks = rows.seq_len // L
    nc = min(max_chunks, n_chunks)
    assert n_chunks % nc == 0
    return L, nc, n_chunks // nc


def _gla(cfg, proj, l, rows, wg, bg, gn, state, prev, out_dtype):
    H, dk, dv = cfg.gla_heads, cfg.dk, cfg.dv
    hk, hv = H * dk, H * dv
    B = rows.n_seq
    L, nc, nt = _scan_blocks(rows, cfg.gla_chunk, 4)
    Tb = L * nc
    assert cfg.off_q % hk == 0 and cfg.off_v % hv == 0
    rb = lambda b, t: b * nt + t
    in_specs = [
        pl.BlockSpec((Tb, hk), lambda b, t: (rb(b, t), cfg.off_q // hk)),
        pl.BlockSpec((Tb, hk), lambda b, t: (rb(b, t), cfg.off_k // hk)),
        pl.BlockSpec((Tb, hv), lambda b, t: (rb(b, t), cfg.off_v // hv)),
        pl.BlockSpec((Tb, LANE), lambda b, t: (rb(b, t), cfg.off_glr // LANE)),
        pl.BlockSpec((Tb, hv), lambda b, t: (rb(b, t), cfg.off_r // hv)),
        pl.BlockSpec((None, LANE, hk), lambda b, t: (l, 0, 0)),
        pl.BlockSpec((None, 1, hk), lambda b, t: (l, 0, 0)),
        pl.BlockSpec((None, 1, dv), lambda b, t: (l, 0, 0)),
    ]
    args = [proj, proj, proj, proj, proj, wg, bg, gn]
    state_spec = pl.BlockSpec((None, None, H, dk, dv), lambda b, t: (l, b, 0, 0, 0))
    if state is not None:
        in_specs.append(state_spec)
        args.append(state)
    aliases = {}
    if prev is not None:
        in_specs.append(pl.BlockSpec(memory_space=pl.ANY))
        aliases = {len(args): 1}
        args.append(prev)
    kern = functools.partial(_gla_kernel, L=L, nc=nc, nt=nt, H=H, zero_init=state is None,
                             has_prev=prev is not None, scale=dk ** -0.5, inv_tau=1.0 / cfg.gla_tau)
    return pl.pallas_call(
        kern,
        grid=(B, nt),
        in_specs=in_specs,
        out_specs=[pl.BlockSpec((Tb, hv), lambda b, t: (rb(b, t), 0)), state_spec],
        out_shape=[
            jax.ShapeDtypeStruct((rows.m, hv), out_dtype),
            jax.ShapeDtypeStruct((cfg.depth, B, H, dk, dv), F32),
        ],
        scratch_shapes=[pltpu.VMEM((H, dk, dv), F32)],
        input_output_aliases=aliases,
        compiler_params=_cparams(("parallel", "arbitrary")),
        name="gla_scan",
    )(*args)


def _head_expander(n_heads, per):
    col_head = jnp.arange(n_heads * per, dtype=jnp.int32) // per
    return (col_head[None, :] == jnp.arange(LANE, dtype=jnp.int32)[:, None]).astype(BF16)


def _ssd(cfg, proj, l, rows, conv_w, conv_b, dtb, alog, dsk, nrm, conv_state, state, prev, out_dtype):
    G, gw, N, P, hpg = cfg.ssd_groups, cfg.gw, cfg.ssd_state, cfg.ssd_head_dim, cfg.hpg
    inner, heads, cdim = cfg.inner, cfg.ssd_heads, cfg.conv_dim
    B = rows.n_seq
    K = cfg.ssd_conv
    L, nc, nt = _scan_blocks(rows, cfg.ssd_chunk, 2)
    Tb = L * nc
    wl = hpg * L
    gn = G * N
    assert cfg.off_z % inner == 0 and cfg.off_xbc % inner == 0 and (cfg.off_xbc + inner) % gn == 0
    rb = lambda b, t: b * nt + t
    in_specs = [
        pl.BlockSpec((Tb, inner), lambda b, t: (rb(b, t), cfg.off_z // inner)),
        pl.BlockSpec((Tb, inner), lambda b, t: (rb(b, t), cfg.off_xbc // inner)),
        pl.BlockSpec((Tb, gn), lambda b, t: (rb(b, t), (cfg.off_xbc + inner) // gn)),
        pl.BlockSpec((Tb, gn), lambda b, t: (rb(b, t), (cfg.off_xbc + inner) // gn + 1)),
        pl.BlockSpec((Tb, LANE), lambda b, t: (rb(b, t), cfg.off_dt // LANE)),
        pl.BlockSpec((None, K, cdim), lambda b, t: (l, 0, 0)),
        pl.BlockSpec((None, 1, cdim), lambda b, t: (l, 0, 0)),
        pl.BlockSpec((None, 1, LANE), lambda b, t: (l, 0, 0)),
        pl.BlockSpec((None, 1, LANE), lambda b, t: (l, 0, 0)),
        pl.BlockSpec((None, 1, inner), lambda b, t: (l, 0, 0)),
        pl.BlockSpec((None, 1, inner), lambda b, t: (l, 0, 0)),
        pl.BlockSpec((LANE, inner), lambda b, t: (0, 0)),
    ]
    args = [proj, proj, proj, proj, proj, conv_w, conv_b, dtb, alog, dsk, nrm, _head_expander(heads, P)]
    has_exl = L != P
    if has_exl:
        in_specs.append(pl.BlockSpec((LANE, heads * L), lambda b, t: (0, 0)))
        args.append(_head_expander(heads, L))
    state_spec = pl.BlockSpec((None, None, heads, P, N), lambda b, t: (l, b, 0, 0, 0))
    if state is not None:
        in_specs += [pl.BlockSpec((None, None, K - 1, cdim), lambda b, t: (l, b, 0, 0)), state_spec]
        args += [conv_state, state]
    aliases = {}
    if prev is not None:
        in_specs.append(pl.BlockSpec(memory_space=pl.ANY))
        aliases = {len(args): 1}
        args.append(prev)
    kern = functools.partial(_ssd_kernel, L=L, nc=nc, nt=nt, G=G, hpg=hpg, P=P, N=N, zero_init=state is None,
                             has_prev=prev is not None, has_exl=has_exl)
    xbd_dtype = BF16 if L % 16 == 0 else F32
    return pl.pallas_call(
        kern,
        grid=(B, nt),
        in_specs=in_specs,
        out_specs=[pl.BlockSpec((Tb, inner), lambda b, t: (rb(b, t), 0)), state_spec],
        out_shape=[
            jax.ShapeDtypeStruct((rows.m, inner), out_dtype),
            jax.ShapeDtypeStruct((cfg.depth, B, heads, P, N), F32),
        ],
        scratch_shapes=[
            pltpu.VMEM((Tb + SUBLANE, cdim), F32),
            pltpu.VMEM((inner, N), F32),
            pltpu.VMEM((G * nc, wl, gw), xbd_dtype),
        ],
        input_output_aliases=aliases,
        compiler_params=_cparams(("parallel", "arbitrary")),
        name="ssd_scan",
    )(*args)


def _pack_w_in(cfg, w_in, tn):
    H, dk, dv = cfg.gla_heads, cfg.dk, cfg.dv
    sizes = (H * dk, H * dk, H * dv, H * dv, cfg.gla_rank, cfg.inner, cfg.conv_dim, cfg.ssd_heads, 2 * cfg.d_model)
    offs = [0]
    for s in sizes:
        offs.append(offs[-1] + s)
    seg = lambda i: w_in[:, :, offs[i]:offs[i + 1]]
    padto = lambda a, n: jnp.pad(a, ((0, 0), (0, 0), (0, n - a.shape[2])))
    n_p = cfg.n_packed(tn)
    parts = [seg(5), seg(6), seg(0), seg(1), seg(2), seg(3), seg(8), padto(seg(4), LANE), padto(seg(7), LANE)]
    packed = jnp.concatenate(parts, axis=2)
    return padto(packed, n_p).astype(BF16)


def _trunk(cfg, x, mod, rows, states, weights, scan_dtype):
    (w_in_p, wg, bg, gn, w_gla_proj, conv_w, conv_b, dtb, alog, dsk, nrm, w_ssd_proj, w_mix_out,
     w_ffn_in, w_ffn_out, final_norm, tn_in) = weights
    s_gla, s_ssm, s_conv = states
    mod_arr = rows.mod_array(mod)
    rows_e = _Rows(rows.n_seq, rows.seq_len, rows.long_seq, tm_max=256)
    B = rows.n_seq
    K = cfg.ssd_conv
    new_gla = new_ssm = None
    new_conv = []
    for l in range(cfg.depth):
        h = _prenorm(x, mod_arr, rows_e, l, 1, 0)
        proj = _in_proj(h, w_in_p, l, rows, tn_in)
        oa, new_gla = _gla(cfg, proj, l, rows, wg, bg, gn, s_gla, new_gla, scan_dtype)
        yb, new_ssm = _ssd(cfg, proj, l, rows, conv_w, conv_b, dtb, alog, dsk, nrm, s_conv, s_ssm, new_ssm,
                           scan_dtype)
        assert rows.seq_len >= K - 1
        tail = proj.reshape(B, rows.seq_len, proj.shape[1])[:, rows.seq_len - (K - 1):, :]
        new_conv.append(tail[:, :, cfg.off_xbc:cfg.off_xbc + cfg.conv_dim])
        merged = _merge(cfg, oa, yb, proj, w_gla_proj, w_ssd_proj, l, rows)
        x = _resid(merged, w_mix_out, x, mod_arr, rows, l, 2, name="mix_out")
        h = _prenorm(x, mod_arr, rows_e, l, 4, 3)
        act = _ffn_in(h, w_ffn_in, l, rows, cfg.d_ff)
        x = _resid(act, w_ffn_out, x, mod_arr, rows, l, 5, name="ffn_out")
    y = _final_norm(x, final_norm, rows_e)
    return y, new_gla, new_ssm, jnp.stack(new_conv)


def _forward(cfg, x_prompt, x_sample, c_prompt, c_sample, state_gla, state_ssm, state_conv,
             w_ada, b_ada, w_in, w_gla_gate, b_gla_gate, gla_norm, w_gla_proj, conv_w, conv_b,
             dt_bias, A_log, d_skip, ssd_norm, w_ssd_proj, w_mix_out, w_ffn_in, w_ffn_out, final_norm):
    depth, d = cfg.depth, cfg.d_model
    bp, tp, _ = x_prompt.shape
    bs, ts, _ = x_sample.shape
    H, dk, dv = cfg.gla_heads, cfg.dk, cfg.dv
    P = cfg.ssd_head_dim
    assert cfg.ssd_heads <= LANE and cfg.gla_rank <= LANE
    assert cfg.off_gates % 256 == 0

    pad = (-bp) % SUBLANE
    c_all = jnp.concatenate([c_prompt, jnp.zeros((pad, d), F32), c_sample], axis=0)
    mod = _ada(c_all, w_ada, b_ada)
    mod_p = mod[:, :bp]
    mod_s = mod[:, bp + pad:]

    tn_in = 768
    padl = lambda a: jnp.pad(a, ((0, 0), (0, LANE - a.shape[1])))
    weights = (
        _pack_w_in(cfg, w_in, tn_in),
        jnp.pad(w_gla_gate, ((0, 0), (0, LANE - cfg.gla_rank), (0, 0))).astype(BF16),
        b_gla_gate.reshape(depth, 1, H * dk),
        gla_norm.reshape(depth, 1, dv),
        w_gla_proj.astype(BF16),
        conv_w,
        conv_b.reshape(depth, 1, cfg.conv_dim),
        padl(dt_bias).reshape(depth, 1, LANE),
        padl(A_log).reshape(depth, 1, LANE),
        jnp.repeat(d_skip, P, axis=1).reshape(depth, 1, cfg.inner),
        ssd_norm.reshape(depth, 1, cfg.inner),
        w_ssd_proj.astype(BF16),
        w_mix_out.astype(BF16),
        w_ffn_in.astype(BF16),
        w_ffn_out.astype(BF16),
        final_norm,
        tn_in,
    )
    rows_p = _Rows(bp, tp, True)
    rows_s = _Rows(bs, ts, False)
    y_p, gla_p, ssm_p, conv_p = _trunk(cfg, x_prompt.reshape(bp * tp, d), mod_p, rows_p,
                                       (None, None, None), weights, BF16)
    states_s = (state_gla, state_ssm, state_conv)
    y_s, gla_s, ssm_s, conv_s = _trunk(cfg, x_sample.reshape(bs * ts, d), mod_s, rows_s,
                                       states_s, weights, F32)
    return (y_p.reshape(bp, tp, d), y_s.reshape(bs, ts, d), gla_p, ssm_p, conv_p, gla_s, ssm_s, conv_s)


def kernel(x_prompt, x_sample, c_prompt, c_sample, state_gla, state_ssm, state_conv, w_ada, b_ada, w_in,
           w_gla_gate, b_gla_gate, gla_norm, w_gla_proj, conv_w, conv_b, dt_bias, A_log, d_skip, ssd_norm,
           w_ssd_proj, w_mix_out, w_ffn_in, w_ffn_out, final_norm):
    cfg = Cfg()
    assert x_prompt.shape[2] == cfg.d_model
    return _forward(cfg, x_prompt, x_sample, c_prompt, c_sample, state_gla, state_ssm, state_conv,
                    w_ada, b_ada, w_in, w_gla_gate, b_gla_gate, gla_norm, w_gla_proj, conv_w, conv_b,
                    dt_bias, A_log, d_skip, ssd_norm, w_ssd_proj, w_mix_out, w_ffn_in, w_ffn_out, final_norm)
```

```python
import dataclasses
import functools

import jax
import jax.numpy as jnp
from jax import lax
from jax.experimental import pallas as pl
from jax.experimental.pallas import tpu as pltpu

F32 = jnp.float32
BF16 = jnp.bfloat16

LANE = 128
SUBLANE = 8
VMEM_LIMIT = 56 * 1024 * 1024
EPS = 1e-6
N_ADA = 6


@dataclasses.dataclass(frozen=True)
class Cfg:
    d_model: int = 2048
    depth: int = 2
    gla_heads: int = 4
    gla_rank: int = 16
    gla_tau: float = 16.0
    gla_chunk: int = 64
    ssd_head_dim: int = 64
    ssd_groups: int = 8
    ssd_state: int = 128
    ssd_conv: int = 4
    ssd_chunk: int = 64

    @property
    def dk(self):
        return self.d_model // 2 // self.gla_heads

    @property
    def dv(self):
        return self.d_model // self.gla_heads

    @property
    def inner(self):
        return 2 * self.d_model

    @property
    def ssd_heads(self):
        return self.inner // self.ssd_head_dim

    @property
    def hpg(self):
        return self.ssd_heads // self.ssd_groups

    @property
    def gw(self):
        return self.inner // self.ssd_groups

    @property
    def conv_dim(self):
        return self.inner + 2 * self.ssd_groups * self.ssd_state

    @property
    def d_ff(self):
        return ((8 * self.d_model // 3 + 255) // 256) * 256

    @property
    def off_z(self):
        return 0

    @property
    def off_xbc(self):
        return self.inner

    @property
    def off_q(self):
        return self.off_xbc + self.conv_dim

    @property
    def off_k(self):
        return self.off_q + self.gla_heads * self.dk

    @property
    def off_v(self):
        return self.off_k + self.gla_heads * self.dk

    @property
    def off_r(self):
        return self.off_v + self.gla_heads * self.dv

    @property
    def off_gates(self):
        return self.off_r + self.gla_heads * self.dv

    @property
    def off_glr(self):
        return self.off_gates + 2 * self.d_model

    @property
    def off_dt(self):
        return self.off_glr + LANE

    def n_packed(self, tn):
        n = self.off_dt + LANE
        return -(-n // tn) * tn


def _cparams(sem):
    return pltpu.CompilerParams(dimension_semantics=sem, vmem_limit_bytes=VMEM_LIMIT)


def _sigmoid(x):
    return 1.0 / (1.0 + jnp.exp(-x))


def _silu(x):
    return x * _sigmoid(x)


def _softplus(x):
    return jnp.maximum(x, 0.0) + jnp.log1p(jnp.exp(-jnp.abs(x)))


def _log_sigmoid(x):
    return jnp.minimum(x, 0.0) - jnp.log1p(jnp.exp(-jnp.abs(x)))


def _dot(a, b):
    return jnp.dot(a, b, preferred_element_type=F32)


def _dot_nt(a, b):
    return lax.dot_general(a, b, (((1,), (1,)), ((), ())), preferred_element_type=F32)


def _dot_tn(a, b):
    return lax.dot_general(a, b, (((0,), (0,)), ((), ())), preferred_element_type=F32)


def _split(x, parts):
    out = []
    r = x
    for i in range(parts):
        p = r.astype(BF16)
        out.append(p)
        if i + 1 < parts:
            r = r - p.astype(F32)
    return out


def _rms(x):
    return x * lax.rsqrt(jnp.mean(x * x, axis=-1, keepdims=True) + EPS)


def _expand_rows(v, reps):
    if reps == 1:
        return v
    g = v.shape[0]
    r = lax.broadcasted_iota(jnp.int32, (g * reps, g), 0)
    c = lax.broadcasted_iota(jnp.int32, (g * reps, g), 1)
    lo = c * reps
    e = jnp.where((r >= lo) & (r < lo + reps), 1.0, 0.0).astype(BF16)
    acc = None
    for p in _split(v, 3):
        d = _dot(e, p)
        acc = d if acc is None else acc + d
    return acc


def _tri(n, strict=False, upper=False):
    r = lax.broadcasted_iota(jnp.int32, (n, n), 0)
    c = lax.broadcasted_iota(jnp.int32, (n, n), 1)
    if upper:
        r, c = c, r
    return (r > c) if strict else (r >= c)


def _ada_kernel(c_ref, w_ref, b_ref, o_ref):
    a = _silu(c_ref[...]).astype(BF16)
    o_ref[...] = _dot(a, w_ref[...].astype(BF16)) + b_ref[...]


NORM_ROWS = 256


def _modulated_norm(x_ref, sc_ref, sh_ref, h_scr, reps):
    @pl.when(pl.program_id(1) == 0)
    def _():
        tm = x_ref.shape[0]
        step = min(NORM_ROWS, tm)
        for r in range(0, tm, step):
            rs = slice(r, r + step)
            if reps == 1:
                sc, sh = sc_ref[...], sh_ref[...]
            else:
                ms = slice(r // reps, (r + step) // reps)
                sc = _expand_rows(sc_ref[ms, :], reps)
                sh = _expand_rows(sh_ref[ms, :], reps)
            h_scr[rs, :] = (_rms(x_ref[rs, :]) * (1.0 + sc) + sh).astype(h_scr.dtype)


def _norm_proj_kernel(x_ref, sc_ref, sh_ref, w_ref, o_ref, h_scr, *, reps):
    _modulated_norm(x_ref, sc_ref, sh_ref, h_scr, reps)
    o_ref[...] = _dot_nt(h_scr[...], w_ref[...])


def _norm_ffn_in_kernel(x_ref, sc_ref, sh_ref, wg_ref, wu_ref, o_ref, h_scr, *, reps):
    _modulated_norm(x_ref, sc_ref, sh_ref, h_scr, reps)
    h = h_scr[...]
    o_ref[...] = (_silu(_dot(h, wg_ref[...])) * _dot(h, wu_ref[...])).astype(o_ref.dtype)


def _final_norm_kernel(x_ref, g_ref, o_ref):
    o_ref[...] = _rms(x_ref[...]) * g_ref[...]


def _merge_kernel(oa_ref, yb_ref, wa_ref, wb_ref, ga_ref, gb_ref, o_ref):
    ya = _dot(oa_ref[...].astype(BF16), wa_ref[...])
    yb = _dot(yb_ref[...].astype(BF16), wb_ref[...])
    o_ref[...] = (_sigmoid(ga_ref[...]) * ya + _sigmoid(gb_ref[...]) * yb).astype(o_ref.dtype)


def _resid_kernel(a_ref, w_ref, x_ref, g_ref, o_ref, *, reps):
    y = _dot(a_ref[...], w_ref[...])
    o_ref[...] = x_ref[...] + _expand_rows(g_ref[...], reps) * y


def _gla_kernel(*refs, L, nc, nt, H, zero_init, has_prev, scale, inv_tau):
    q_ref, k_ref, v_ref, glr_ref, r_ref, wg_ref, bg_ref, gn_ref = refs[:8]
    i = 8
    s0_ref = None
    if not zero_init:
        s0_ref = refs[i]
        i += 1
    if has_prev:
        i += 1
    o_ref, s_out_ref, s_scr = refs[i:i + 3]
    t = pl.program_id(1)
    _, dk, dv = s_scr.shape

    @pl.when(t == 0)
    def _init():
        if zero_init:
            s_scr[...] = jnp.zeros_like(s_scr)
        else:
            s_scr[...] = s0_ref[...]

    ga = _dot(glr_ref[...].astype(BF16), wg_ref[...]) + bg_ref[...]
    log_a = _log_sigmoid(ga) * inv_tau
    tri = jnp.where(_tri(L), 1.0, 0.0).astype(BF16)
    mask = _tri(L)
    ones = jnp.ones((L, LANE), BF16)
    heads = range(H)
    ck = [slice(h * dk, (h + 1) * dk) for h in heads]
    cv = [slice(h * dv, (h + 1) * dv) for h in heads]
    for c in range(nc):
        sl = slice(c * L, (c + 1) * L)
        g_parts = _split(log_a[sl, :], 2)
        b = _dot(tri, g_parts[0]) + _dot(tri, g_parts[1])
        tot = [_dot_tn(g_parts[0][:, ck[h]], ones) + _dot_tn(g_parts[1][:, ck[h]], ones) for h in heads]
        eb = jnp.exp(b)
        q_in = (q_ref[sl, :] * scale * eb).astype(BF16)
        k = k_ref[sl, :]
        k_in = (k * jnp.exp(-b)).astype(BF16)
        k_out = (k * jnp.exp(b[L - 1:L, :] - b)).astype(BF16)
        v = [v_ref[sl, cv[h]].astype(BF16) for h in heads]
        att = [jnp.where(mask, _dot_nt(q_in[:, ck[h]], k_in[:, ck[h]]), 0.0).astype(BF16) for h in heads]
        s = [s_scr[h] for h in heads]
        o = [_dot(q_in[:, ck[h]], s[h].astype(BF16)) + _dot(att[h], v[h]) for h in heads]
        upd = [_dot_tn(k_out[:, ck[h]], v[h]) for h in heads]
        for h in heads:
            decay = jnp.exp(tot[h])
            s_scr[h] = s[h] * jnp.concatenate([decay] * (dv // LANE), axis=1) + upd[h]
            on = _rms(o[h]) * gn_ref[...]
            o_ref[sl, cv[h]] = (on * _silu(r_ref[sl, cv[h]])).astype(o_ref.dtype)

    @pl.when(t == nt - 1)
    def _fin():
        s_out_ref[...] = s_scr[...]


def _ssd_kernel(*refs, L, nc, nt, G, hpg, P, N, zero_init, has_prev, has_exl):
    (z_ref, x_ref, b_ref, c_ref, dt_ref, cw_ref, cb_ref, dtb_ref, alog_ref, dsk_ref, nrm_ref, exp_ref) = refs[:12]
    i = 12
    exl_ref = exp_ref
    if has_exl:
        exl_ref = refs[i]
        i += 1
    cs_ref = h0_ref = None
    if not zero_init:
        cs_ref, h0_ref = refs[i:i + 2]
        i += 2
    if has_prev:
        i += 1
    y_ref, h_out_ref, u, h_scr, xbd = refs[i:i + 5]
    t = pl.program_id(1)
    Tb = L * nc
    gw = hpg * P
    inner = G * gw
    wl = hpg * L
    nconv = cw_ref.shape[0]
    halo = SUBLANE

    @pl.when(t == 0)
    def _init():
        u[0:halo, :] = jnp.zeros((halo, u.shape[1]), F32)
        if zero_init:
            h_scr[...] = jnp.zeros_like(h_scr)
        else:
            u[halo - (nconv - 1):halo, :] = cs_ref[...]
            h_scr[...] = h0_ref[...].reshape(inner, N)
        xbd[...] = jnp.zeros_like(xbd)

    u[halo:halo + Tb, 0:inner] = x_ref[...]
    u[halo:halo + Tb, inner:inner + G * N] = b_ref[...]
    u[halo:halo + Tb, inner + G * N:inner + 2 * G * N] = c_ref[...]

    def conv(r0, lo, width):
        acc = cb_ref[:, lo:lo + width]
        for j in range(nconv):
            acc = acc + u[pl.ds(r0 + halo - (nconv - 1) + j, L), lo:lo + width] * cw_ref[j:j + 1, lo:lo + width]
        return _silu(acc)

    dtp = _softplus(dt_ref[...] + dtb_ref[...])
    a_h = dtp * (-jnp.exp(alog_ref[...]))
    dtp_parts = _split(dtp, 2)
    a_parts = _split(a_h, 2)

    def expand(parts, ex):
        return _dot(parts[0], ex) + _dot(parts[1], ex)

    tri = jnp.where(_tri(L), 1.0, 0.0).astype(BF16)
    ones = jnp.ones((L, LANE), BF16)
    row_l = lax.broadcasted_iota(jnp.int32, (L, G * wl), 0)
    s_l = lax.broadcasted_iota(jnp.int32, (L, G * wl), 1) % L
    later = row_l > s_l
    causal = row_l >= s_l
    early_cast = L % 16 == 0
    lane_in_piece = lax.broadcasted_iota(jnp.int32, (L, LANE), 1)
    groups = range(G)
    gc = [slice(g * gw, (g + 1) * gw) for g in groups]

    for c in range(nc):
        sl = slice(c * L, (c + 1) * L)
        ex_all = exp_ref[...]
        dt_exp = expand([p[sl] for p in dtp_parts], ex_all)
        a_exp_p = expand([p[sl] for p in a_parts], ex_all)
        a_exp_l = expand([p[sl] for p in a_parts], exl_ref[...]) if has_exl else a_exp_p
        ap = _split(a_exp_p, 2)
        al = _split(jnp.where(later, a_exp_l, 0.0), 2)
        ccol = _dot(tri, ap[0]) + _dot(tri, ap[1])
        seg = _dot(tri, al[0]) + _dot(tri, al[1])
        decay = jnp.where(causal, jnp.exp(seg), 0.0)
        grow = jnp.exp(ccol)
        fade = jnp.exp(ccol[L - 1:L, :] - ccol)

        xs, bs, w_cats, inters = [], [], [], []
        for g in groups:
            x = conv(c * L, g * gw, gw)
            b = conv(c * L, inner + g * N, N)
            cm = conv(c * L, inner + G * N + g * N, N).astype(BF16)
            if early_cast:
                b_t = jnp.concatenate([b.astype(BF16)] * hpg, axis=0)
            else:
                b_t = jnp.concatenate([b] * hpg, axis=0).astype(BF16)
            w_cats.append((_dot_nt(cm, b_t) * decay[:, g * wl:(g + 1) * wl]).astype(BF16))
            inters.append(_dot_nt(cm, h_scr[gc[g], :].astype(BF16)))
            xs.append(x)
            bs.append(b)

        xts, intras = [], []
        for g in groups:
            xt = xs[g] * dt_exp[:, gc[g]]
            xt_c = xt.astype(xbd.dtype)
            slab = g * nc + c
            for hh in range(hpg):
                lo = (hh * P) // LANE * LANE
                piece = xt_c[:, lo:lo + LANE]
                own = (lane_in_piece >= hh * P - lo) & (lane_in_piece < (hh + 1) * P - lo)
                xbd[slab, hh * L:(hh + 1) * L, lo:lo + LANE] = jnp.where(own, piece, jnp.zeros_like(piece))
            intras.append(_dot(w_cats[g], xbd[slab].astype(BF16)))
            xts.append(xt)

        for g in groups:
            x_dl = (xts[g] * fade[:, gc[g]]).astype(BF16)
            tot = _dot_tn(ap[0][:, gc[g]], ones) + _dot_tn(ap[1][:, gc[g]], ones)
            h = h_scr[gc[g], :]
            h_scr[gc[g], :] = (h * jnp.concatenate([jnp.exp(tot)] * (N // LANE), axis=1)
                               + _dot_tn(x_dl, bs[g].astype(BF16)))

        for g in groups:
            y = intras[g] + grow[:, gc[g]] * inters[g] + dsk_ref[:, gc[g]] * xs[g]
            yg = y * _silu(z_ref[sl, gc[g]])
            y_ref[sl, gc[g]] = (_rms(yg) * nrm_ref[:, gc[g]]).astype(y_ref.dtype)

    u[0:halo, :] = u[Tb:Tb + halo, :]

    @pl.when(t == nt - 1)
    def _fin():
        h_out_ref[...] = h_scr[...].reshape(h_out_ref.shape)


def _ada(c_all, w_ada, b_ada, tn=512):
    depth, d, n = w_ada.shape
    m = c_all.shape[0]
    return pl.pallas_call(
        _ada_kernel,
        grid=(depth, n // tn),
        in_specs=[
            pl.BlockSpec((m, d), lambda l, j: (0, 0)),
            pl.BlockSpec((None, d, tn), lambda l, j: (l, 0, j)),
            pl.BlockSpec((None, 1, tn), lambda l, j: (l, 0, j)),
        ],
        out_specs=pl.BlockSpec((None, m, tn), lambda l, j: (l, 0, j)),
        out_shape=jax.ShapeDtypeStruct((depth, m, n), F32),
        compiler_params=_cparams(("parallel", "parallel")),
        name="ada",
    )(c_all, w_ada, b_ada.reshape(depth, 1, n))


class _Rows:
    def __init__(self, n_seq, seq_len, long_seq, tm_max=1024):
        self.n_seq, self.seq_len, self.long_seq = n_seq, seq_len, long_seq
        self.m = n_seq * seq_len
        if long_seq:
            self.tm = min(tm_max, seq_len)
            assert seq_len % self.tm == 0
            self.reps = 1
            self.mod_rows = 1
        else:
            self.tm = min(self.m, tm_max)
            assert self.m % self.tm == 0 and self.tm % seq_len == 0
            self.reps = seq_len
            self.mod_rows = self.tm // seq_len
        self.n_tiles = self.m // self.tm

    def lhs_spec(self, width):
        if self.n_tiles == 1:
            return pl.BlockSpec((self.tm, width), lambda i, j: (i, 0), pipeline_mode=pl.Buffered(1))
        return pl.BlockSpec((self.tm, width), lambda i, j: (i, 0))

    def mod_array(self, mod):
        depth, _, n = mod.shape
        if self.reps == 1:
            return mod.reshape(depth, self.n_seq, 1, n)
        return mod

    def mod_spec(self, l, tn, col_block):
        if self.reps == 1:
            per = self.seq_len // self.tm
            return pl.BlockSpec((None, None, 1, tn), lambda i, j: (l, i // per, 0, col_block(j)))
        return pl.BlockSpec((None, self.mod_rows, tn), lambda i, j: (l, i, col_block(j)))


def _final_norm(x, gain, rows):
    m, d = x.shape
    return pl.pallas_call(
        _final_norm_kernel,
        grid=(rows.n_tiles,),
        in_specs=[pl.BlockSpec((rows.tm, d), lambda i: (i, 0)), pl.BlockSpec((1, d), lambda i: (0, 0))],
        out_specs=pl.BlockSpec((rows.tm, d), lambda i: (i, 0)),
        out_shape=jax.ShapeDtypeStruct((m, d), F32),
        compiler_params=_cparams(("parallel",)),
        name="final_norm",
    )(x, gain.reshape(1, d))


def _in_proj(x, mod_arr, w_t, l, rows, tn, v_scale, v_shift):
    m, d = x.shape
    n = w_t.shape[1]
    return pl.pallas_call(
        functools.partial(_norm_proj_kernel, reps=rows.reps),
        grid=(rows.n_tiles, n // tn),
        in_specs=[
            rows.lhs_spec(d),
            rows.mod_spec(l, d, lambda j: v_scale),
            rows.mod_spec(l, d, lambda j: v_shift),
            pl.BlockSpec((None, tn, d), lambda i, j: (l, j, 0)),
        ],
        out_specs=pl.BlockSpec((rows.tm, tn), lambda i, j: (i, j)),
        out_shape=jax.ShapeDtypeStruct((m, n), F32),
        scratch_shapes=[pltpu.VMEM((rows.tm, d), BF16)],
        compiler_params=_cparams(("parallel", "arbitrary")),
        name="in_proj",
    )(x, mod_arr, mod_arr, w_t)


def _merge(cfg, oa, yb, proj, wa, wb, l, rows):
    m = oa.shape[0]
    d = cfg.d_model
    tn = 512 if oa.dtype == BF16 else 256
    ga0 = cfg.off_gates // tn
    gb0 = (cfg.off_gates + d) // tn
    return pl.pallas_call(
        _merge_kernel,
        grid=(rows.n_tiles, d // tn),
        in_specs=[
            rows.lhs_spec(oa.shape[1]),
            rows.lhs_spec(yb.shape[1]),
            pl.BlockSpec((None, wa.shape[1], tn), lambda i, j: (l, 0, j)),
            pl.BlockSpec((None, wb.shape[1], tn), lambda i, j: (l, 0, j)),
            pl.BlockSpec((rows.tm, tn), lambda i, j: (i, ga0 + j)),
            pl.BlockSpec((rows.tm, tn), lambda i, j: (i, gb0 + j)),
        ],
        out_specs=pl.BlockSpec((rows.tm, tn), lambda i, j: (i, j)),
        out_shape=jax.ShapeDtypeStruct((m, d), BF16),
        compiler_params=_cparams(("parallel", "arbitrary")),
        name="merge",
    )(oa, yb, wa, wb, proj, proj)


def _resid(a, w, x, mod_arr, rows, l, v_gate, tn=512, name="resid"):
    m, kdim = a.shape
    d = x.shape[1]
    per = d // tn
    return pl.pallas_call(
        functools.partial(_resid_kernel, reps=rows.reps),
        grid=(rows.n_tiles, d // tn),
        in_specs=[
            rows.lhs_spec(kdim),
            pl.BlockSpec((None, kdim, tn), lambda i, j: (l, 0, j)),
            pl.BlockSpec((rows.tm, tn), lambda i, j: (i, j)),
            rows.mod_spec(l, tn, lambda j: v_gate * per + j),
        ],
        out_specs=pl.BlockSpec((rows.tm, tn), lambda i, j: (i, j)),
        out_shape=jax.ShapeDtypeStruct((m, d), F32),
        compiler_params=_cparams(("parallel", "arbitrary")),
        name=name,
    )(a, w, x, mod_arr)


def _ffn_in(x, mod_arr, w, l, rows, d_ff, v_scale, v_shift, tn=512):
    m, d = x.shape
    nb = d_ff // tn
    return pl.pallas_call(
        functools.partial(_norm_ffn_in_kernel, reps=rows.reps),
        grid=(rows.n_tiles, nb),
        in_specs=[
            rows.lhs_spec(d),
            rows.mod_spec(l, d, lambda j: v_scale),
            rows.mod_spec(l, d, lambda j: v_shift),
            pl.BlockSpec((None, d, tn), lambda i, j: (l, 0, j)),
            pl.BlockSpec((None, d, tn), lambda i, j: (l, 0, nb + j)),
        ],
        out_specs=pl.BlockSpec((rows.tm, tn), lambda i, j: (i, j)),
        out_shape=jax.ShapeDtypeStruct((m, d_ff), BF16),
        scratch_shapes=[pltpu.VMEM((rows.tm, d), BF16)],
        compiler_params=_cparams(("parallel", "arbitrary")),
        name="ffn_in",
    )(x, mod_arr, mod_arr, w, w)


def _scan_blocks(rows, chunk, max_chunks):
    L = min(chunk, rows.seq_len)
    assert rows.seq_len % L == 0
    n_chunks = rows.seq_len // L
    nc = min(max_chunks, n_chunks)
    assert n_chunks % nc == 0
    return L, nc, n_chunks // nc


def _gla(cfg, proj, l, rows, wg, bg, gn, state, prev, out_dtype):
    H, dk, dv = cfg.gla_heads, cfg.dk, cfg.dv
    hk, hv = H * dk, H * dv
    B = rows.n_seq
    L, nc, nt = _scan_blocks(rows, cfg.gla_chunk, 4)
    Tb = L * nc
    assert cfg.off_q % hk == 0 and cfg.off_v % hv == 0
    rb = lambda b, t: b * nt + t
    in_specs = [
        pl.BlockSpec((Tb, hk), lambda b, t: (rb(b, t), cfg.off_q // hk)),
        pl.BlockSpec((Tb, hk), lambda b, t: (rb(b, t), cfg.off_k // hk)),
        pl.BlockSpec((Tb, hv), lambda b, t: (rb(b, t), cfg.off_v // hv)),
        pl.BlockSpec((Tb, LANE), lambda b, t: (rb(b, t), cfg.off_glr // LANE)),
        pl.BlockSpec((Tb, hv), lambda b, t: (rb(b, t), cfg.off_r // hv)),
        pl.BlockSpec((None, LANE, hk), lambda b, t: (l, 0, 0)),
        pl.BlockSpec((None, 1, hk), lambda b, t: (l, 0, 0)),
        pl.BlockSpec((None, 1, dv), lambda b, t: (l, 0, 0)),
    ]
    args = [proj, proj, proj, proj, proj, wg, bg, gn]
    state_spec = pl.BlockSpec((None, None, H, dk, dv), lambda b, t: (l, b, 0, 0, 0))
    if state is not None:
        in_specs.append(state_spec)
        args.append(state)
    aliases = {}
    if prev is not None:
        in_specs.append(pl.BlockSpec(memory_space=pl.ANY))
        aliases = {len(args): 1}
        args.append(prev)
    kern = functools.partial(_gla_kernel, L=L, nc=nc, nt=nt, H=H, zero_init=state is None,
                             has_prev=prev is not None, scale=dk ** -0.5, inv_tau=1.0 / cfg.gla_tau)
    return pl.pallas_call(
        kern,
        grid=(B, nt),
        in_specs=in_specs,
        out_specs=[pl.BlockSpec((Tb, hv), lambda b, t: (rb(b, t), 0)), state_spec],
        out_shape=[
            jax.ShapeDtypeStruct((rows.m, hv), out_dtype),
            jax.ShapeDtypeStruct((cfg.depth, B, H, dk, dv), F32),
        ],
        scratch_shapes=[pltpu.VMEM((H, dk, dv), F32)],
        input_output_aliases=aliases,
        compiler_params=_cparams(("parallel", "arbitrary")),
        name="gla_scan",
    )(*args)


def _head_expander(n_heads, per):
    col_head = jnp.arange(n_heads * per, dtype=jnp.int32) // per
    return (col_head[None, :] == jnp.arange(LANE, dtype=jnp.int32)[:, None]).astype(BF16)


def _ssd(cfg, proj, l, rows, conv_w, conv_b, dtb, alog, dsk, nrm, conv_state, state, prev, out_dtype):
    G, gw, N, P, hpg = cfg.ssd_groups, cfg.gw, cfg.ssd_state, cfg.ssd_head_dim, cfg.hpg
    inner, heads, cdim = cfg.inner, cfg.ssd_heads, cfg.conv_dim
    B = rows.n_seq
    K = cfg.ssd_conv
    L, nc, nt = _scan_blocks(rows, cfg.ssd_chunk, 2)
    Tb = L * nc
    wl = hpg * L
    gn = G * N
    assert cfg.off_z % inner == 0 and cfg.off_xbc % inner == 0 and (cfg.off_xbc + inner) % gn == 0
    rb = lambda b, t: b * nt + t
    in_specs = [
        pl.BlockSpec((Tb, inner), lambda b, t: (rb(b, t), cfg.off_z // inner)),
        pl.BlockSpec((Tb, inner), lambda b, t: (rb(b, t), cfg.off_xbc // inner)),
        pl.BlockSpec((Tb, gn), lambda b, t: (rb(b, t), (cfg.off_xbc + inner) // gn)),
        pl.BlockSpec((Tb, gn), lambda b, t: (rb(b, t), (cfg.off_xbc + inner) // gn + 1)),
        pl.BlockSpec((Tb, LANE), lambda b, t: (rb(b, t), cfg.off_dt // LANE)),
        pl.BlockSpec((None, K, cdim), lambda b, t: (l, 0, 0)),
        pl.BlockSpec((None, 1, cdim), lambda b, t: (l, 0, 0)),
        pl.BlockSpec((None, 1, LANE), lambda b, t: (l, 0, 0)),
        pl.BlockSpec((None, 1, LANE), lambda b, t: (l, 0, 0)),
        pl.BlockSpec((None, 1, inner), lambda b, t: (l, 0, 0)),
        pl.BlockSpec((None, 1, inner), lambda b, t: (l, 0, 0)),
        pl.BlockSpec((LANE, inner), lambda b, t: (0, 0)),
    ]
    args = [proj, proj, proj, proj, proj, conv_w, conv_b, dtb, alog, dsk, nrm, _head_expander(heads, P)]
    has_exl = L != P
    if has_exl:
        in_specs.append(pl.BlockSpec((LANE, heads * L), lambda b, t: (0, 0)))
        args.append(_head_expander(heads, L))
    state_spec = pl.BlockSpec((None, None, heads, P, N), lambda b, t: (l, b, 0, 0, 0))
    if state is not None:
        in_specs += [pl.BlockSpec((None, None, K - 1, cdim), lambda b, t: (l, b, 0, 0)), state_spec]
        args += [conv_state, state]
    aliases = {}
    if prev is not None:
        in_specs.append(pl.BlockSpec(memory_space=pl.ANY))
        aliases = {len(args): 1}
        args.append(prev)
    kern = functools.partial(_ssd_kernel, L=L, nc=nc, nt=nt, G=G, hpg=hpg, P=P, N=N, zero_init=state is None,
                             has_prev=prev is not None, has_exl=has_exl)
    xbd_dtype = BF16 if L % 16 == 0 else F32
    return pl.pallas_call(
        kern,
        grid=(B, nt),
        in_specs=in_specs,
        out_specs=[pl.BlockSpec((Tb, inner), lambda b, t: (rb(b, t), 0)), state_spec],
        out_shape=[
            jax.ShapeDtypeStruct((rows.m, inner), out_dtype),
            jax.ShapeDtypeStruct((cfg.depth, B, heads, P, N), F32),
        ],
        scratch_shapes=[
            pltpu.VMEM((Tb + SUBLANE, cdim), F32),
            pltpu.VMEM((inner, N), F32),
            pltpu.VMEM((G * nc, wl, gw), xbd_dtype),
        ],
        input_output_aliases=aliases,
        compiler_params=_cparams(("parallel", "arbitrary")),
        name="ssd_scan",
    )(*args)


def _pack_w_in(cfg, w_in, tn):
    H, dk, dv = cfg.gla_heads, cfg.dk, cfg.dv
    sizes = (H * dk, H * dk, H * dv, H * dv, cfg.gla_rank, cfg.inner, cfg.conv_dim, cfg.ssd_heads, 2 * cfg.d_model)
    offs = [0]
    for s in sizes:
        offs.append(offs[-1] + s)
    w_t = jnp.swapaxes(w_in, 1, 2)
    seg = lambda i: w_t[:, offs[i]:offs[i + 1], :].astype(BF16)
    padto = lambda a, n: jnp.pad(a, ((0, 0), (0, n - a.shape[1]), (0, 0)))
    n_p = cfg.n_packed(tn)
    parts = [seg(5), seg(6), seg(0), seg(1), seg(2), seg(3), seg(8), padto(seg(4), LANE), padto(seg(7), LANE)]
    return padto(jnp.concatenate(parts, axis=1), n_p)


def _trunk(cfg, x, mod, rows, states, weights, scan_dtype):
    (w_in_p, wg, bg, gn, w_gla_proj, conv_w, conv_b, dtb, alog, dsk, nrm, w_ssd_proj, w_mix_out,
     w_ffn_in, w_ffn_out, final_norm, tn_in) = weights
    s_gla, s_ssm, s_conv = states
    mod_arr = rows.mod_array(mod)
    rows_e = _Rows(rows.n_seq, rows.seq_len, rows.long_seq, tm_max=256)
    B = rows.n_seq
    K = cfg.ssd_conv
    new_gla = new_ssm = None
    new_conv = []
    for l in range(cfg.depth):
        proj = _in_proj(x, mod_arr, w_in_p, l, rows, tn_in, 1, 0)
        oa, new_gla = _gla(cfg, proj, l, rows, wg, bg, gn, s_gla, new_gla, scan_dtype)
        yb, new_ssm = _ssd(cfg, proj, l, rows, conv_w, conv_b, dtb, alog, dsk, nrm, s_conv, s_ssm, new_ssm,
                           scan_dtype)
        assert rows.seq_len >= K - 1
        tail = proj.reshape(B, rows.seq_len, proj.shape[1])[:, rows.seq_len - (K - 1):, :]
        new_conv.append(tail[:, :, cfg.off_xbc:cfg.off_xbc + cfg.conv_dim])
        merged = _merge(cfg, oa, yb, proj, w_gla_proj, w_ssd_proj, l, rows)
        x = _resid(merged, w_mix_out, x, mod_arr, rows, l, 2, name="mix_out")
        act = _ffn_in(x, mod_arr, w_ffn_in, l, rows, cfg.d_ff, 4, 3)
        x = _resid(act, w_ffn_out, x, mod_arr, rows, l, 5, name="ffn_out")
    y = _final_norm(x, final_norm, rows_e)
    return y, new_gla, new_ssm, jnp.stack(new_conv)


def _forward(cfg, x_prompt, x_sample, c_prompt, c_sample, state_gla, state_ssm, state_conv,
             w_ada, b_ada, w_in, w_gla_gate, b_gla_gate, gla_norm, w_gla_proj, conv_w, conv_b,
             dt_bias, A_log, d_skip, ssd_norm, w_ssd_proj, w_mix_out, w_ffn_in, w_ffn_out, final_norm):
    depth, d = cfg.depth, cfg.d_model
    bp, tp, _ = x_prompt.shape
    bs, ts, _ = x_sample.shape
    H, dk, dv = cfg.gla_heads, cfg.dk, cfg.dv
    P = cfg.ssd_head_dim
    assert cfg.ssd_heads <= LANE and cfg.gla_rank <= LANE
    assert cfg.off_gates % 256 == 0

    pad = (-bp) % SUBLANE
    c_all = jnp.concatenate([c_prompt, jnp.zeros((pad, d), F32), c_sample], axis=0)
    mod = _ada(c_all, w_ada, b_ada)
    mod_p = mod[:, :bp]
    mod_s = mod[:, bp + pad:]

    tn_in = 768
    padl = lambda a: jnp.pad(a, ((0, 0), (0, LANE - a.shape[1])))
    weights = (
        _pack_w_in(cfg, w_in, tn_in),
        jnp.pad(w_gla_gate, ((0, 0), (0, LANE - cfg.gla_rank), (0, 0))).astype(BF16),
        b_gla_gate.reshape(depth, 1, H * dk),
        gla_norm.reshape(depth, 1, dv),
        w_gla_proj.astype(BF16),
        conv_w,
        conv_b.reshape(depth, 1, cfg.conv_dim),
        padl(dt_bias).reshape(depth, 1, LANE),
        padl(A_log).reshape(depth, 1, LANE),
        jnp.repeat(d_skip, P, axis=1).reshape(depth, 1, cfg.inner),
        ssd_norm.reshape(depth, 1, cfg.inner),
        w_ssd_proj.astype(BF16),
        w_mix_out.astype(BF16),
        w_ffn_in.astype(BF16),
        w_ffn_out.astype(BF16),
        final_norm,
        tn_in,
    )
    rows_p = _Rows(bp, tp, True)
    rows_s = _Rows(bs, ts, False)
    y_p, gla_p, ssm_p, conv_p = _trunk(cfg, x_prompt.reshape(bp * tp, d), mod_p, rows_p,
                                       (None, None, None), weights, BF16)
    states_s = (state_gla, state_ssm, state_conv)
    y_s, gla_s, ssm_s, conv_s = _trunk(cfg, x_sample.reshape(bs * ts, d), mod_s, rows_s,
                                       states_s, weights, F32)
    return (y_p.reshape(bp, tp, d), y_s.reshape(bs, ts, d), gla_p, ssm_p, conv_p, gla_s, ssm_s, conv_s)


def kernel(x_prompt, x_sample, c_prompt, c_sample, state_gla, state_ssm, state_conv, w_ada, b_ada, w_in,
           w_gla_gate, b_gla_gate, gla_norm, w_gla_proj, conv_w, conv_b, dt_bias, A_log, d_skip, ssd_norm,
           w_ssd_proj, w_mix_out, w_ffn_in, w_ffn_out, final_norm):
    cfg = Cfg()
    assert x_prompt.shape[2] == cfg.d_model
    return _forward(cfg, x_prompt, x_sample, c_prompt, c_sample, state_gla, state_ssm, state_conv,
                    w_ada, b_ada, w_in, w_gla_gate, b_gla_gate, gla_norm, w_gla_proj, conv_w, conv_b,
                    dt_bias, A_log, d_skip, ssd_norm, w_ssd_proj, w_mix_out, w_ffn_in, w_ffn_out, final_norm)
```

```python
import dataclasses
import functools

import jax
import jax.numpy as jnp
from jax import lax
from jax.experimental import pallas as pl
from jax.experimental.pallas import tpu as pltpu

F32 = jnp.float32
BF16 = jnp.bfloat16

LANE = 128
SUBLANE = 8
VMEM_LIMIT = 56 * 1024 * 1024
EPS = 1e-6
N_ADA = 6


@dataclasses.dataclass(frozen=True)
class Cfg:
    d_model: int = 2048
    depth: int = 2
    gla_heads: int = 4
    gla_rank: int = 16
    gla_tau: float = 16.0
    gla_chunk: int = 64
    ssd_head_dim: int = 64
    ssd_groups: int = 8
    ssd_state: int = 128
    ssd_conv: int = 4
    ssd_chunk: int = 64

    @property
    def dk(self):
        return self.d_model // 2 // self.gla_heads

    @property
    def dv(self):
        return self.d_model // self.gla_heads

    @property
    def inner(self):
        return 2 * self.d_model

    @property
    def ssd_heads(self):
        return self.inner // self.ssd_head_dim

    @property
    def hpg(self):
        return self.ssd_heads // self.ssd_groups

    @property
    def gw(self):
        return self.inner // self.ssd_groups

    @property
    def conv_dim(self):
        return self.inner + 2 * self.ssd_groups * self.ssd_state

    @property
    def d_ff(self):
        return ((8 * self.d_model // 3 + 255) // 256) * 256

    @property
    def off_z(self):
        return 0

    @property
    def off_xbc(self):
        return self.inner

    @property
    def off_q(self):
        return self.off_xbc + self.conv_dim

    @property
    def off_k(self):
        return self.off_q + self.gla_heads * self.dk

    @property
    def off_v(self):
        return self.off_k + self.gla_heads * self.dk

    @property
    def off_r(self):
        return self.off_v + self.gla_heads * self.dv

    @property
    def off_gates(self):
        return self.off_r + self.gla_heads * self.dv

    @property
    def off_glr(self):
        return self.off_gates + 2 * self.d_model

    @property
    def off_dt(self):
        return self.off_glr + LANE

    def n_packed(self, tn):
        n = self.off_dt + LANE
        return -(-n // tn) * tn


def _cparams(sem):
    return pltpu.CompilerParams(dimension_semantics=sem, vmem_limit_bytes=VMEM_LIMIT)


def _sigmoid(x):
    return 1.0 / (1.0 + jnp.exp(-x))


def _silu(x):
    return x * _sigmoid(x)


def _softplus(x):
    return jnp.maximum(x, 0.0) + jnp.log1p(jnp.exp(-jnp.abs(x)))


def _log_sigmoid(x):
    return jnp.minimum(x, 0.0) - jnp.log1p(jnp.exp(-jnp.abs(x)))


def _dot(a, b):
    return jnp.dot(a, b, preferred_element_type=F32)


def _dot_nt(a, b):
    return lax.dot_general(a, b, (((1,), (1,)), ((), ())), preferred_element_type=F32)


def _dot_tn(a, b):
    return lax.dot_general(a, b, (((0,), (0,)), ((), ())), preferred_element_type=F32)


def _split(x, parts):
    out = []
    r = x
    for i in range(parts):
        p = r.astype(BF16)
        out.append(p)
        if i + 1 < parts:
            r = r - p.astype(F32)
    return out


def _split_rows(x):
    hi = x.astype(BF16).astype(F32)
    return jnp.concatenate([hi, x - hi], axis=0).astype(BF16)


def _split_lanes(x):
    hi = x.astype(BF16)
    return jnp.concatenate([hi, (x - hi.astype(F32)).astype(BF16)], axis=1)


def _rms(x):
    return x * lax.rsqrt(jnp.mean(x * x, axis=-1, keepdims=True) + EPS)


def _expand_rows(v, reps):
    if reps == 1:
        return v
    g = v.shape[0]
    r = lax.broadcasted_iota(jnp.int32, (g * reps, g), 0)
    c = lax.broadcasted_iota(jnp.int32, (g * reps, g), 1)
    lo = c * reps
    e = jnp.where((r >= lo) & (r < lo + reps), 1.0, 0.0).astype(BF16)
    acc = None
    for p in _split(v, 3):
        d = _dot(e, p)
        acc = d if acc is None else acc + d
    return acc


def _tri(n, strict=False, upper=False):
    r = lax.broadcasted_iota(jnp.int32, (n, n), 0)
    c = lax.broadcasted_iota(jnp.int32, (n, n), 1)
    if upper:
        r, c = c, r
    return (r > c) if strict else (r >= c)


def _ada_kernel(c_ref, w_ref, b_ref, o_ref):
    a = _silu(c_ref[...]).astype(BF16)
    o_ref[...] = _dot(a, w_ref[...].astype(BF16)) + b_ref[...]


NORM_ROWS = 256


def _modulated_norm(x_ref, sc_ref, sh_ref, h_scr, reps):
    @pl.when(pl.program_id(1) == 0)
    def _():
        tm = x_ref.shape[0]
        step = min(NORM_ROWS, tm)
        for r in range(0, tm, step):
            rs = slice(r, r + step)
            if reps == 1:
                sc, sh = sc_ref[...], sh_ref[...]
            else:
                ms = slice(r // reps, (r + step) // reps)
                sc = _expand_rows(sc_ref[ms, :], reps)
                sh = _expand_rows(sh_ref[ms, :], reps)
            h_scr[rs, :] = (_rms(x_ref[rs, :]) * (1.0 + sc) + sh).astype(h_scr.dtype)


def _norm_proj_kernel(x_ref, sc_ref, sh_ref, w_ref, o_ref, h_scr, *, reps):
    _modulated_norm(x_ref, sc_ref, sh_ref, h_scr, reps)
    o_ref[...] = _dot_nt(h_scr[...], w_ref[...])


def _norm_ffn_in_kernel(x_ref, sc_ref, sh_ref, wg_ref, wu_ref, o_ref, h_scr, *, reps):
    _modulated_norm(x_ref, sc_ref, sh_ref, h_scr, reps)
    h = h_scr[...]
    o_ref[...] = (_silu(_dot(h, wg_ref[...])) * _dot(h, wu_ref[...])).astype(o_ref.dtype)


def _final_norm_kernel(x_ref, g_ref, o_ref):
    o_ref[...] = _rms(x_ref[...]) * g_ref[...]


def _merge_kernel(oa_ref, yb_ref, wa_ref, wb_ref, ga_ref, gb_ref, o_ref):
    ya = _dot(oa_ref[...].astype(BF16), wa_ref[...])
    yb = _dot(yb_ref[...].astype(BF16), wb_ref[...])
    o_ref[...] = (_sigmoid(ga_ref[...]) * ya + _sigmoid(gb_ref[...]) * yb).astype(o_ref.dtype)


def _resid_kernel(a_ref, w_ref, x_ref, g_ref, o_ref, *, reps):
    y = _dot(a_ref[...], w_ref[...])
    o_ref[...] = x_ref[...] + _expand_rows(g_ref[...], reps) * y


def _gla_kernel(*refs, L, nc, nt, ns, H, zero_init, has_prev, scale, inv_tau):
    q_ref, k_ref, v_ref, glr_ref, r_ref, wg_ref, bg_ref, gn_ref = refs[:8]
    i = 8
    s0_ref = None
    if not zero_init:
        s0_ref = refs[i]
        i += 1
    if has_prev:
        i += 1
    o_ref, s_out_ref, s_scr = refs[i:i + 3]
    t = pl.program_id(1)
    _, dk, dv = s_scr.shape

    @pl.when(t == 0)
    def _init():
        if zero_init:
            s_scr[...] = jnp.zeros_like(s_scr)
        else:
            s_scr[...] = s0_ref[...].reshape(s_scr.shape)

    ga = _dot(glr_ref[...].astype(BF16), wg_ref[...]) + bg_ref[...]
    log_a = _log_sigmoid(ga) * inv_tau
    tri = jnp.where(_tri(L), 1.0, 0.0).astype(BF16)
    tri2 = jnp.concatenate([tri, tri], axis=1)
    mask = _tri(L)
    ones2 = jnp.ones((2 * L, LANE), BF16)
    heads = range(H)
    ck = [slice(h * dk, (h + 1) * dk) for h in heads]
    cv = [slice(h * dv, (h + 1) * dv) for h in heads]
    for unit in range(ns * nc):
        sl = slice(unit * L, (unit + 1) * L)
        s0 = (unit // nc) * H
        g2 = _split_rows(log_a[sl, :])
        b = _dot(tri2, g2)
        tot = [_dot_tn(g2[:, ck[h]], ones2) for h in heads]
        eb = jnp.exp(b)
        q_in = (q_ref[sl, :] * scale * eb).astype(BF16)
        k = k_ref[sl, :]
        k_in = (k * jnp.exp(-b)).astype(BF16)
        k_out = (k * jnp.exp(b[L - 1:L, :] - b)).astype(BF16)
        v = [v_ref[sl, cv[h]].astype(BF16) for h in heads]
        att = [jnp.where(mask, _dot_nt(q_in[:, ck[h]], k_in[:, ck[h]]), 0.0).astype(BF16) for h in heads]
        s = [s_scr[s0 + h] for h in heads]
        o = [_dot(q_in[:, ck[h]], s[h].astype(BF16)) + _dot(att[h], v[h]) for h in heads]
        upd = [_dot_tn(k_out[:, ck[h]], v[h]) for h in heads]
        for h in heads:
            decay = jnp.exp(tot[h])
            s_scr[s0 + h] = s[h] * jnp.concatenate([decay] * (dv // LANE), axis=1) + upd[h]
            on = _rms(o[h]) * gn_ref[...]
            o_ref[sl, cv[h]] = (on * _silu(r_ref[sl, cv[h]])).astype(o_ref.dtype)

    @pl.when(t == nt - 1)
    def _fin():
        s_out_ref[...] = s_scr[...].reshape(s_out_ref.shape)


def _ssd_kernel(*refs, L, nc, nt, G, hpg, P, N, zero_init, has_prev, has_exl):
    (z_ref, x_ref, b_ref, c_ref, dt_ref, cw_ref, cb_ref, dtb_ref, alog_ref, dsk_ref, nrm_ref, exp_ref) = refs[:12]
    i = 12
    exl_ref = exp_ref
    if has_exl:
        exl_ref = refs[i]
        i += 1
    cs_ref = h0_ref = None
    if not zero_init:
        cs_ref, h0_ref = refs[i:i + 2]
        i += 2
    if has_prev:
        i += 1
    y_ref, h_out_ref, u, h_scr, xbd = refs[i:i + 5]
    t = pl.program_id(1)
    Tb = L * nc
    gw = hpg * P
    inner = G * gw
    wl = hpg * L
    nconv = cw_ref.shape[0]
    halo = SUBLANE

    @pl.when(t == 0)
    def _init():
        u[0:halo, :] = jnp.zeros((halo, u.shape[1]), F32)
        if zero_init:
            h_scr[...] = jnp.zeros_like(h_scr)
        else:
            u[halo - (nconv - 1):halo, :] = cs_ref[...]
            h_scr[...] = h0_ref[...].reshape(inner, N)
        xbd[...] = jnp.zeros_like(xbd)

    u[halo:halo + Tb, 0:inner] = x_ref[...]
    u[halo:halo + Tb, inner:inner + G * N] = b_ref[...]
    u[halo:halo + Tb, inner + G * N:inner + 2 * G * N] = c_ref[...]

    def conv(r0, lo, width):
        acc = cb_ref[:, lo:lo + width]
        for j in range(nconv):
            acc = acc + u[pl.ds(r0 + halo - (nconv - 1) + j, L), lo:lo + width] * cw_ref[j:j + 1, lo:lo + width]
        return _silu(acc)

    dtp = _softplus(dt_ref[...] + dtb_ref[...])
    a_h = dtp * (-jnp.exp(alog_ref[...]))

    tri = jnp.where(_tri(L), 1.0, 0.0).astype(BF16)
    tri2 = jnp.concatenate([tri, tri], axis=1)
    ones2 = jnp.ones((2 * L, LANE), BF16)
    row_l = lax.broadcasted_iota(jnp.int32, (L, G * wl), 0)
    s_l = lax.broadcasted_iota(jnp.int32, (L, G * wl), 1) % L
    later = row_l > s_l
    causal = row_l >= s_l
    early_cast = L % 16 == 0
    lane_in_piece = lax.broadcasted_iota(jnp.int32, (L, LANE), 1)
    groups = range(G)
    gc = [slice(g * gw, (g + 1) * gw) for g in groups]

    for c in range(nc):
        sl = slice(c * L, (c + 1) * L)
        da2 = _split_lanes(jnp.concatenate([dtp[sl], a_h[sl]], axis=0))
        da_exp = _dot(da2, exp_ref[...])
        dt_exp = da_exp[:L]
        a_exp_p = da_exp[L:]
        a_exp_l = _dot(_split_lanes(a_h[sl]), exl_ref[...]) if has_exl else a_exp_p
        ap2 = _split_rows(a_exp_p)
        al2 = _split_rows(jnp.where(later, a_exp_l, 0.0))
        ccol = _dot(tri2, ap2)
        seg = _dot(tri2, al2)
        decay = jnp.where(causal, jnp.exp(seg), 0.0)
        grow = jnp.exp(ccol)
        fade = jnp.exp(ccol[L - 1:L, :] - ccol)

        xs, bs, w_cats, inters = [], [], [], []
        for g in groups:
            x = conv(c * L, g * gw, gw)
            b = conv(c * L, inner + g * N, N)
            cm = conv(c * L, inner + G * N + g * N, N).astype(BF16)
            if early_cast:
                b_t = jnp.concatenate([b.astype(BF16)] * hpg, axis=0)
            else:
                b_t = jnp.concatenate([b] * hpg, axis=0).astype(BF16)
            w_cats.append((_dot_nt(cm, b_t) * decay[:, g * wl:(g + 1) * wl]).astype(BF16))
            inters.append(_dot_nt(cm, h_scr[gc[g], :].astype(BF16)))
            xs.append(x)
            bs.append(b)

        xts, intras = [], []
        for g in groups:
            xt = xs[g] * dt_exp[:, gc[g]]
            xt_c = xt.astype(xbd.dtype)
            slab = g * nc + c
            for hh in range(hpg):
                lo = (hh * P) // LANE * LANE
                piece = xt_c[:, lo:lo + LANE]
                own = (lane_in_piece >= hh * P - lo) & (lane_in_piece < (hh + 1) * P - lo)
                xbd[slab, hh * L:(hh + 1) * L, lo:lo + LANE] = jnp.where(own, piece, jnp.zeros_like(piece))
            intras.append(_dot(w_cats[g], xbd[slab].astype(BF16)))
            xts.append(xt)

        for g in groups:
            x_dl = (xts[g] * fade[:, gc[g]]).astype(BF16)
            tot = _dot_tn(ap2[:, gc[g]], ones2)
            h = h_scr[gc[g], :]
            h_scr[gc[g], :] = (h * jnp.concatenate([jnp.exp(tot)] * (N // LANE), axis=1)
                               + _dot_tn(x_dl, bs[g].astype(BF16)))

        for g in groups:
            y = intras[g] + grow[:, gc[g]] * inters[g] + dsk_ref[:, gc[g]] * xs[g]
            yg = y * _silu(z_ref[sl, gc[g]])
            y_ref[sl, gc[g]] = (_rms(yg) * nrm_ref[:, gc[g]]).astype(y_ref.dtype)

    u[0:halo, :] = u[Tb:Tb + halo, :]

    @pl.when(t == nt - 1)
    def _fin():
        h_out_ref[...] = h_scr[...].reshape(h_out_ref.shape)


def _ada(c_all, w_ada, b_ada, tn=512):
    depth, d, n = w_ada.shape
    m = c_all.shape[0]
    return pl.pallas_call(
        _ada_kernel,
        grid=(depth, n // tn),
        in_specs=[
            pl.BlockSpec((m, d), lambda l, j: (0, 0)),
            pl.BlockSpec((None, d, tn), lambda l, j: (l, 0, j)),
            pl.BlockSpec((None, 1, tn), lambda l, j: (l, 0, j)),
        ],
        out_specs=pl.BlockSpec((None, m, tn), lambda l, j: (l, 0, j)),
        out_shape=jax.ShapeDtypeStruct((depth, m, n), F32),
        compiler_params=_cparams(("parallel", "parallel")),
        name="ada",
    )(c_all, w_ada, b_ada.reshape(depth, 1, n))


class _Rows:
    def __init__(self, n_seq, seq_len, long_seq, tm_max=1024):
        self.n_seq, self.seq_len, self.long_seq = n_seq, seq_len, long_seq
        self.m = n_seq * seq_len
        if long_seq:
            self.tm = min(tm_max, seq_len)
            assert seq_len % self.tm == 0
            self.reps = 1
            self.mod_rows = 1
        else:
            self.tm = min(self.m, tm_max)
            assert self.m % self.tm == 0 and self.tm % seq_len == 0
            self.reps = seq_len
            self.mod_rows = self.tm // seq_len
        self.n_tiles = self.m // self.tm

    def lhs_spec(self, width):
        if self.n_tiles == 1:
            return pl.BlockSpec((self.tm, width), lambda i, j: (i, 0), pipeline_mode=pl.Buffered(1))
        return pl.BlockSpec((self.tm, width), lambda i, j: (i, 0))

    def mod_array(self, mod):
        depth, _, n = mod.shape
        if self.reps == 1:
            return mod.reshape(depth, self.n_seq, 1, n)
        return mod

    def mod_spec(self, l, tn, col_block):
        if self.reps == 1:
            per = self.seq_len // self.tm
            return pl.BlockSpec((None, None, 1, tn), lambda i, j: (l, i // per, 0, col_block(j)))
        return pl.BlockSpec((None, self.mod_rows, tn), lambda i, j: (l, i, col_block(j)))


def _final_norm(x, gain, rows):
    m, d = x.shape
    return pl.pallas_call(
        _final_norm_kernel,
        grid=(rows.n_tiles,),
        in_specs=[pl.BlockSpec((rows.tm, d), lambda i: (i, 0)), pl.BlockSpec((1, d), lambda i: (0, 0))],
        out_specs=pl.BlockSpec((rows.tm, d), lambda i: (i, 0)),
        out_shape=jax.ShapeDtypeStruct((m, d), F32),
        compiler_params=_cparams(("parallel",)),
        name="final_norm",
    )(x, gain.reshape(1, d))


def _in_proj(x, mod_arr, w_t, l, rows, tn, v_scale, v_shift):
    m, d = x.shape
    n = w_t.shape[1]
    return pl.pallas_call(
        functools.partial(_norm_proj_kernel, reps=rows.reps),
        grid=(rows.n_tiles, n // tn),
        in_specs=[
            rows.lhs_spec(d),
            rows.mod_spec(l, d, lambda j: v_scale),
            rows.mod_spec(l, d, lambda j: v_shift),
            pl.BlockSpec((None, tn, d), lambda i, j: (l, j, 0)),
        ],
        out_specs=pl.BlockSpec((rows.tm, tn), lambda i, j: (i, j)),
        out_shape=jax.ShapeDtypeStruct((m, n), F32),
        scratch_shapes=[pltpu.VMEM((rows.tm, d), BF16)],
        compiler_params=_cparams(("parallel", "arbitrary")),
        name="in_proj",
    )(x, mod_arr, mod_arr, w_t)


def _merge(cfg, oa, yb, proj, wa, wb, l, rows):
    m = oa.shape[0]
    d = cfg.d_model
    tn = 512 if oa.dtype == BF16 else 256
    ga0 = cfg.off_gates // tn
    gb0 = (cfg.off_gates + d) // tn
    return pl.pallas_call(
        _merge_kernel,
        grid=(rows.n_tiles, d // tn),
        in_specs=[
            rows.lhs_spec(oa.shape[1]),
            rows.lhs_spec(yb.shape[1]),
            pl.BlockSpec((None, wa.shape[1], tn), lambda i, j: (l, 0, j)),
            pl.BlockSpec((None, wb.shape[1], tn), lambda i, j: (l, 0, j)),
            pl.BlockSpec((rows.tm, tn), lambda i, j: (i, ga0 + j)),
            pl.BlockSpec((rows.tm, tn), lambda i, j: (i, gb0 + j)),
        ],
        out_specs=pl.BlockSpec((rows.tm, tn), lambda i, j: (i, j)),
        out_shape=jax.ShapeDtypeStruct((m, d), BF16),
        compiler_params=_cparams(("parallel", "arbitrary")),
        name="merge",
    )(oa, yb, wa, wb, proj, proj)


def _resid(a, w, x, mod_arr, rows, l, v_gate, tn=512, name="resid"):
    m, kdim = a.shape
    d = x.shape[1]
    per = d // tn
    return pl.pallas_call(
        functools.partial(_resid_kernel, reps=rows.reps),
        grid=(rows.n_tiles, d // tn),
        in_specs=[
            rows.lhs_spec(kdim),
            pl.BlockSpec((None, kdim, tn), lambda i, j: (l, 0, j)),
            pl.BlockSpec((rows.tm, tn), lambda i, j: (i, j)),
            rows.mod_spec(l, tn, lambda j: v_gate * per + j),
        ],
        out_specs=pl.BlockSpec((rows.tm, tn), lambda i, j: (i, j)),
        out_shape=jax.ShapeDtypeStruct((m, d), F32),
        compiler_params=_cparams(("parallel", "arbitrary")),
        name=name,
    )(a, w, x, mod_arr)


def _ffn_in(x, mod_arr, w, l, rows, d_ff, v_scale, v_shift, tn=512):
    m, d = x.shape
    nb = d_ff // tn
    return pl.pallas_call(
        functools.partial(_norm_ffn_in_kernel, reps=rows.reps),
        grid=(rows.n_tiles, nb),
        in_specs=[
            rows.lhs_spec(d),
            rows.mod_spec(l, d, lambda j: v_scale),
            rows.mod_spec(l, d, lambda j: v_shift),
            pl.BlockSpec((None, d, tn), lambda i, j: (l, 0, j)),
            pl.BlockSpec((None, d, tn), lambda i, j: (l, 0, nb + j)),
        ],
        out_specs=pl.BlockSpec((rows.tm, tn), lambda i, j: (i, j)),
        out_shape=jax.ShapeDtypeStruct((m, d_ff), BF16),
        scratch_shapes=[pltpu.VMEM((rows.tm, d), BF16)],
        compiler_params=_cparams(("parallel", "arbitrary")),
        name="ffn_in",
    )(x, mod_arr, mod_arr, w, w)


def _scan_blocks(rows, chunk, max_chunks):
    L = min(chunk, rows.seq_len)
    assert rows.seq_len % L == 0
    n_chunks = rows.seq_len // L
    nc = min(max_chunks, n_chunks)
    assert n_chunks % nc == 0
    return L, nc, n_chunks // nc


def _gla(cfg, proj, l, rows, wg, bg, gn, state, prev, out_dtype):
    H, dk, dv = cfg.gla_heads, cfg.dk, cfg.dv
    hk, hv = H * dk, H * dv
    B = rows.n_seq
    L, nc, nt = _scan_blocks(rows, cfg.gla_chunk, 4)
    ns = 2 if (nt == 1 and nc == 1 and B % 2 == 0) else 1
    Tb = ns * L * nc
    assert cfg.off_q % hk == 0 and cfg.off_v % hv == 0
    rb = lambda b, t: b * nt + t
    in_specs = [
        pl.BlockSpec((Tb, hk), lambda b, t: (rb(b, t), cfg.off_q // hk)),
        pl.BlockSpec((Tb, hk), lambda b, t: (rb(b, t), cfg.off_k // hk)),
        pl.BlockSpec((Tb, hv), lambda b, t: (rb(b, t), cfg.off_v // hv)),
        pl.BlockSpec((Tb, LANE), lambda b, t: (rb(b, t), cfg.off_glr // LANE)),
        pl.BlockSpec((Tb, hv), lambda b, t: (rb(b, t), cfg.off_r // hv)),
        pl.BlockSpec((None, LANE, hk), lambda b, t: (l, 0, 0)),
        pl.BlockSpec((None, 1, hk), lambda b, t: (l, 0, 0)),
        pl.BlockSpec((None, 1, dv), lambda b, t: (l, 0, 0)),
    ]
    args = [proj, proj, proj, proj, proj, wg, bg, gn]
    state_spec = pl.BlockSpec((None, ns, H, dk, dv), lambda b, t: (l, b, 0, 0, 0))
    if state is not None:
        in_specs.append(state_spec)
        args.append(state)
    aliases = {}
    if prev is not None:
        in_specs.append(pl.BlockSpec(memory_space=pl.ANY))
        aliases = {len(args): 1}
        args.append(prev)
    kern = functools.partial(_gla_kernel, L=L, nc=nc, nt=nt, ns=ns, H=H, zero_init=state is None,
                             has_prev=prev is not None, scale=dk ** -0.5, inv_tau=1.0 / cfg.gla_tau)
    return pl.pallas_call(
        kern,
        grid=(B // ns, nt),
        in_specs=in_specs,
        out_specs=[pl.BlockSpec((Tb, hv), lambda b, t: (rb(b, t), 0)), state_spec],
        out_shape=[
            jax.ShapeDtypeStruct((rows.m, hv), out_dtype),
            jax.ShapeDtypeStruct((cfg.depth, B, H, dk, dv), F32),
        ],
        scratch_shapes=[pltpu.VMEM((ns * H, dk, dv), F32)],
        input_output_aliases=aliases,
        compiler_params=_cparams(("parallel", "arbitrary")),
        name="gla_scan",
    )(*args)


def _head_expander(n_heads, per):
    col_head = jnp.arange(n_heads * per, dtype=jnp.int32) // per
    ex = (col_head[None, :] == jnp.arange(LANE, dtype=jnp.int32)[:, None]).astype(BF16)
    return jnp.concatenate([ex, ex], axis=0)


def _ssd(cfg, proj, l, rows, conv_w, conv_b, dtb, alog, dsk, nrm, conv_state, state, prev, out_dtype):
    G, gw, N, P, hpg = cfg.ssd_groups, cfg.gw, cfg.ssd_state, cfg.ssd_head_dim, cfg.hpg
    inner, heads, cdim = cfg.inner, cfg.ssd_heads, cfg.conv_dim
    B = rows.n_seq
    K = cfg.ssd_conv
    L, nc, nt = _scan_blocks(rows, cfg.ssd_chunk, 2)
    Tb = L * nc
    wl = hpg * L
    gn = G * N
    assert cfg.off_z % inner == 0 and cfg.off_xbc % inner == 0 and (cfg.off_xbc + inner) % gn == 0
    rb = lambda b, t: b * nt + t
    in_specs = [
        pl.BlockSpec((Tb, inner), lambda b, t: (rb(b, t), cfg.off_z // inner)),
        pl.BlockSpec((Tb, inner), lambda b, t: (rb(b, t), cfg.off_xbc // inner)),
        pl.BlockSpec((Tb, gn), lambda b, t: (rb(b, t), (cfg.off_xbc + inner) // gn)),
        pl.BlockSpec((Tb, gn), lambda b, t: (rb(b, t), (cfg.off_xbc + inner) // gn + 1)),
        pl.BlockSpec((Tb, LANE), lambda b, t: (rb(b, t), cfg.off_dt // LANE)),
        pl.BlockSpec((None, K, cdim), lambda b, t: (l, 0, 0)),
        pl.BlockSpec((None, 1, cdim), lambda b, t: (l, 0, 0)),
        pl.BlockSpec((None, 1, LANE), lambda b, t: (l, 0, 0)),
        pl.BlockSpec((None, 1, LANE), lambda b, t: (l, 0, 0)),
        pl.BlockSpec((None, 1, inner), lambda b, t: (l, 0, 0)),
        pl.BlockSpec((None, 1, inner), lambda b, t: (l, 0, 0)),
        pl.BlockSpec((2 * LANE, inner), lambda b, t: (0, 0)),
    ]
    args = [proj, proj, proj, proj, proj, conv_w, conv_b, dtb, alog, dsk, nrm, _head_expander(heads, P)]
    has_exl = L != P
    if has_exl:
        in_specs.append(pl.BlockSpec((2 * LANE, heads * L), lambda b, t: (0, 0)))
        args.append(_head_expander(heads, L))
    state_spec = pl.BlockSpec((None, None, heads, P, N), lambda b, t: (l, b, 0, 0, 0))
    if state is not None:
        in_specs += [pl.BlockSpec((None, None, K - 1, cdim), lambda b, t: (l, b, 0, 0)), state_spec]
        args += [conv_state, state]
    aliases = {}
    if prev is not None:
        in_specs.append(pl.BlockSpec(memory_space=pl.ANY))
        aliases = {len(args): 1}
        args.append(prev)
    kern = functools.partial(_ssd_kernel, L=L, nc=nc, nt=nt, G=G, hpg=hpg, P=P, N=N, zero_init=state is None,
                             has_prev=prev is not None, has_exl=has_exl)
    xbd_dtype = BF16 if L % 16 == 0 else F32
    return pl.pallas_call(
        kern,
        grid=(B, nt),
        in_specs=in_specs,
        out_specs=[pl.BlockSpec((Tb, inner), lambda b, t: (rb(b, t), 0)), state_spec],
        out_shape=[
            jax.ShapeDtypeStruct((rows.m, inner), out_dtype),
            jax.ShapeDtypeStruct((cfg.depth, B, heads, P, N), F32),
        ],
        scratch_shapes=[
            pltpu.VMEM((Tb + SUBLANE, cdim), F32),
            pltpu.VMEM((inner, N), F32),
            pltpu.VMEM((G * nc, wl, gw), xbd_dtype),
        ],
        input_output_aliases=aliases,
        compiler_params=_cparams(("parallel", "arbitrary")),
        name="ssd_scan",
    )(*args)


def _pack_w_in(cfg, w_in, tn):
    H, dk, dv = cfg.gla_heads, cfg.dk, cfg.dv
    sizes = (H * dk, H * dk, H * dv, H * dv, cfg.gla_rank, cfg.inner, cfg.conv_dim, cfg.ssd_heads, 2 * cfg.d_model)
    offs = [0]
    for s in sizes:
        offs.append(offs[-1] + s)
    w_t = jnp.swapaxes(w_in, 1, 2)
    seg = lambda i: w_t[:, offs[i]:offs[i + 1], :].astype(BF16)
    padto = lambda a, n: jnp.pad(a, ((0, 0), (0, n - a.shape[1]), (0, 0)))
    n_p = cfg.n_packed(tn)
    parts = [seg(5), seg(6), seg(0), seg(1), seg(2), seg(3), seg(8), padto(seg(4), LANE), padto(seg(7), LANE)]
    return padto(jnp.concatenate(parts, axis=1), n_p)


def _trunk(cfg, x, mod, rows, states, weights, scan_dtype):
    (w_in_p, wg, bg, gn, w_gla_proj, conv_w, conv_b, dtb, alog, dsk, nrm, w_ssd_proj, w_mix_out,
     w_ffn_in, w_ffn_out, final_norm, tn_in) = weights
    s_gla, s_ssm, s_conv = states
    mod_arr = rows.mod_array(mod)
    rows_e = _Rows(rows.n_seq, rows.seq_len, rows.long_seq, tm_max=256)
    B = rows.n_seq
    K = cfg.ssd_conv
    new_gla = new_ssm = None
    new_conv = []
    for l in range(cfg.depth):
        proj = _in_proj(x, mod_arr, w_in_p, l, rows, tn_in, 1, 0)
        oa, new_gla = _gla(cfg, proj, l, rows, wg, bg, gn, s_gla, new_gla, scan_dtype)
        yb, new_ssm = _ssd(cfg, proj, l, rows, conv_w, conv_b, dtb, alog, dsk, nrm, s_conv, s_ssm, new_ssm,
                           scan_dtype)
        assert rows.seq_len >= K - 1
        tail = proj.reshape(B, rows.seq_len, proj.shape[1])[:, rows.seq_len - (K - 1):, :]
        new_conv.append(tail[:, :, cfg.off_xbc:cfg.off_xbc + cfg.conv_dim])
        merged = _merge(cfg, oa, yb, proj, w_gla_proj, w_ssd_proj, l, rows)
        x = _resid(merged, w_mix_out, x, mod_arr, rows, l, 2, name="mix_out")
        act = _ffn_in(x, mod_arr, w_ffn_in, l, rows, cfg.d_ff, 4, 3)
        x = _resid(act, w_ffn_out, x, mod_arr, rows, l, 5, name="ffn_out")
    y = _final_norm(x, final_norm, rows_e)
    return y, new_gla, new_ssm, jnp.stack(new_conv)


def _forward(cfg, x_prompt, x_sample, c_prompt, c_sample, state_gla, state_ssm, state_conv,
             w_ada, b_ada, w_in, w_gla_gate, b_gla_gate, gla_norm, w_gla_proj, conv_w, conv_b,
             dt_bias, A_log, d_skip, ssd_norm, w_ssd_proj, w_mix_out, w_ffn_in, w_ffn_out, final_norm):
    depth, d = cfg.depth, cfg.d_model
    bp, tp, _ = x_prompt.shape
    bs, ts, _ = x_sample.shape
    H, dk, dv = cfg.gla_heads, cfg.dk, cfg.dv
    P = cfg.ssd_head_dim
    assert cfg.ssd_heads <= LANE and cfg.gla_rank <= LANE
    assert cfg.off_gates % 256 == 0

    pad = (-bp) % SUBLANE
    c_all = jnp.concatenate([c_prompt, jnp.zeros((pad, d), F32), c_sample], axis=0)
    mod = _ada(c_all, w_ada, b_ada)
    mod_p = mod[:, :bp]
    mod_s = mod[:, bp + pad:]

    tn_in = 768
    padl = lambda a: jnp.pad(a, ((0, 0), (0, LANE - a.shape[1])))
    weights = (
        _pack_w_in(cfg, w_in, tn_in),
        jnp.pad(w_gla_gate, ((0, 0), (0, LANE - cfg.gla_rank), (0, 0))).astype(BF16),
        b_gla_gate.reshape(depth, 1, H * dk),
        gla_norm.reshape(depth, 1, dv),
        w_gla_proj.astype(BF16),
        conv_w,
        conv_b.reshape(depth, 1, cfg.conv_dim),
        padl(dt_bias).reshape(depth, 1, LANE),
        padl(A_log).reshape(depth, 1, LANE),
        jnp.repeat(d_skip, P, axis=1).reshape(depth, 1, cfg.inner),
        ssd_norm.reshape(depth, 1, cfg.inner),
        w_ssd_proj.astype(BF16),
        w_mix_out.astype(BF16),
        w_ffn_in.astype(BF16),
        w_ffn_out.astype(BF16),
        final_norm,
        tn_in,
    )
    rows_p = _Rows(bp, tp, True)
    rows_s = _Rows(bs, ts, False)
    y_p, gla_p, ssm_p, conv_p = _trunk(cfg, x_prompt.reshape(bp * tp, d), mod_p, rows_p,
                                       (None, None, None), weights, BF16)
    states_s = (state_gla, state_ssm, state_conv)
    y_s, gla_s, ssm_s, conv_s = _trunk(cfg, x_sample.reshape(bs * ts, d), mod_s, rows_s,
                                       states_s, weights, F32)
    return (y_p.reshape(bp, tp, d), y_s.reshape(bs, ts, d), gla_p, ssm_p, conv_p, gla_s, ssm_s, conv_s)


def kernel(x_prompt, x_sample, c_prompt, c_sample, state_gla, state_ssm, state_conv, w_ada, b_ada, w_in,
           w_gla_gate, b_gla_gate, gla_norm, w_gla_proj, conv_w, conv_b, dt_bias, A_log, d_skip, ssd_norm,
           w_ssd_proj, w_mix_out, w_ffn_in, w_ffn_out, final_norm):
    cfg = Cfg()
    assert x_prompt.shape[2] == cfg.d_model
    return _forward(cfg, x_prompt, x_sample, c_prompt, c_sample, state_gla, state_ssm, state_conv,
                    w_ada, b_ada, w_in, w_gla_gate, b_gla_gate, gla_norm, w_gla_proj, conv_w, conv_b,
                    dt_bias, A_log, d_skip, ssd_norm, w_ssd_proj, w_mix_out, w_ffn_in, w_ffn_out, final_norm)
```

```python
import dataclasses
import functools

import jax
import jax.numpy as jnp
from jax import lax
from jax.experimental import pallas as pl
from jax.experimental.pallas import tpu as pltpu

F32 = jnp.float32
BF16 = jnp.bfloat16

LANE = 128
SUBLANE = 8
VMEM_LIMIT = 56 * 1024 * 1024
EPS = 1e-6
N_ADA = 6


@dataclasses.dataclass(frozen=True)
class Cfg:
    d_model: int = 2048
    depth: int = 2
    gla_heads: int = 4
    gla_rank: int = 16
    gla_tau: float = 16.0
    gla_chunk: int = 64
    ssd_head_dim: int = 64
    ssd_groups: int = 8
    ssd_state: int = 128
    ssd_conv: int = 4
    ssd_chunk: int = 64

    @property
    def dk(self):
        return self.d_model // 2 // self.gla_heads

    @property
    def dv(self):
        return self.d_model // self.gla_heads

    @property
    def inner(self):
        return 2 * self.d_model

    @property
    def ssd_heads(self):
        return self.inner // self.ssd_head_dim

    @property
    def hpg(self):
        return self.ssd_heads // self.ssd_groups

    @property
    def gw(self):
        return self.inner // self.ssd_groups

    @property
    def conv_dim(self):
        return self.inner + 2 * self.ssd_groups * self.ssd_state

    @property
    def d_ff(self):
        return ((8 * self.d_model // 3 + 255) // 256) * 256

    @property
    def off_z(self):
        return 0

    @property
    def off_xbc(self):
        return self.inner

    @property
    def off_q(self):
        return self.off_xbc + self.conv_dim

    @property
    def off_k(self):
        return self.off_q + self.gla_heads * self.dk

    @property
    def off_v(self):
        return self.off_k + self.gla_heads * self.dk

    @property
    def off_r(self):
        return self.off_v + self.gla_heads * self.dv

    @property
    def off_gates(self):
        return self.off_r + self.gla_heads * self.dv

    @property
    def off_glr(self):
        return self.off_gates + 2 * self.d_model

    @property
    def off_dt(self):
        return self.off_glr + LANE

    def n_packed(self, tn):
        n = self.off_dt + LANE
        return -(-n // tn) * tn


def _cparams(sem):
    return pltpu.CompilerParams(dimension_semantics=sem, vmem_limit_bytes=VMEM_LIMIT)


def _sigmoid(x):
    return 1.0 / (1.0 + jnp.exp(-x))


def _silu(x):
    hx = 0.5 * x
    return hx + hx * jnp.tanh(hx)


def _softplus(x):
    return jnp.maximum(x, 0.0) + jnp.log1p(jnp.exp(-jnp.abs(x)))


def _log_sigmoid(x):
    return jnp.minimum(x, 0.0) - jnp.log1p(jnp.exp(-jnp.abs(x)))


def _dot(a, b):
    return jnp.dot(a, b, preferred_element_type=F32)


def _dot_nt(a, b):
    return lax.dot_general(a, b, (((1,), (1,)), ((), ())), preferred_element_type=F32)


def _dot_tn(a, b):
    return lax.dot_general(a, b, (((0,), (0,)), ((), ())), preferred_element_type=F32)


def _split(x, parts):
    out = []
    r = x
    for i in range(parts):
        p = r.astype(BF16)
        out.append(p)
        if i + 1 < parts:
            r = r - p.astype(F32)
    return out


def _split_rows(x):
    hi = x.astype(BF16).astype(F32)
    return jnp.concatenate([hi, x - hi], axis=0).astype(BF16)


def _split_lanes(x):
    hi = x.astype(BF16)
    return jnp.concatenate([hi, (x - hi.astype(F32)).astype(BF16)], axis=1)


def _rms(x):
    return x * lax.rsqrt(jnp.mean(x * x, axis=-1, keepdims=True) + EPS)


def _expand_rows(v, reps):
    if reps == 1:
        return v
    g = v.shape[0]
    r = lax.broadcasted_iota(jnp.int32, (g * reps, g), 0)
    c = lax.broadcasted_iota(jnp.int32, (g * reps, g), 1)
    lo = c * reps
    e = jnp.where((r >= lo) & (r < lo + reps), 1.0, 0.0).astype(BF16)
    acc = None
    for p in _split(v, 3):
        d = _dot(e, p)
        acc = d if acc is None else acc + d
    return acc


def _tri(n, strict=False, upper=False):
    r = lax.broadcasted_iota(jnp.int32, (n, n), 0)
    c = lax.broadcasted_iota(jnp.int32, (n, n), 1)
    if upper:
        r, c = c, r
    return (r > c) if strict else (r >= c)


def _ada_kernel(c_ref, w_ref, b_ref, o_ref):
    a = _silu(c_ref[...]).astype(BF16)
    o_ref[...] = _dot(a, w_ref[...].astype(BF16)) + b_ref[...]


NORM_ROWS = 256


def _modulated_norm(x_ref, sc_ref, sh_ref, h_scr, reps):
    @pl.when(pl.program_id(1) == 0)
    def _():
        tm = x_ref.shape[0]
        step = min(NORM_ROWS, tm)
        for r in range(0, tm, step):
            rs = slice(r, r + step)
            if reps == 1:
                sc, sh = sc_ref[...], sh_ref[...]
            else:
                ms = slice(r // reps, (r + step) // reps)
                sc = _expand_rows(sc_ref[ms, :], reps)
                sh = _expand_rows(sh_ref[ms, :], reps)
            h_scr[rs, :] = (_rms(x_ref[rs, :]) * (1.0 + sc) + sh).astype(h_scr.dtype)


def _norm_proj_kernel(x_ref, sc_ref, sh_ref, w_ref, o_ref, h_scr, *, reps):
    _modulated_norm(x_ref, sc_ref, sh_ref, h_scr, reps)
    o_ref[...] = _dot_nt(h_scr[...], w_ref[...])


def _norm_ffn_in_kernel(x_ref, sc_ref, sh_ref, wg_ref, wu_ref, o_ref, h_scr, *, reps):
    _modulated_norm(x_ref, sc_ref, sh_ref, h_scr, reps)
    h = h_scr[...]
    o_ref[...] = (_silu(_dot(h, wg_ref[...])) * _dot(h, wu_ref[...])).astype(o_ref.dtype)


def _final_norm_kernel(x_ref, g_ref, o_ref):
    o_ref[...] = _rms(x_ref[...]) * g_ref[...]


def _merge_kernel(oa_ref, yb_ref, wa_ref, wb_ref, ga_ref, gb_ref, o_ref):
    ya = _dot(oa_ref[...].astype(BF16), wa_ref[...])
    yb = _dot(yb_ref[...].astype(BF16), wb_ref[...])
    o_ref[...] = (_sigmoid(ga_ref[...]) * ya + _sigmoid(gb_ref[...]) * yb).astype(o_ref.dtype)


def _resid_kernel(a_ref, w_ref, x_ref, g_ref, o_ref, *, reps):
    y = _dot(a_ref[...], w_ref[...])
    o_ref[...] = x_ref[...] + _expand_rows(g_ref[...], reps) * y


def _gla_kernel(*refs, L, nc, nt, ns, H, zero_init, has_prev, scale, inv_tau):
    q_ref, k_ref, v_ref, glr_ref, r_ref, wg_ref, bg_ref, gn_ref = refs[:8]
    i = 8
    s0_ref = None
    if not zero_init:
        s0_ref = refs[i]
        i += 1
    if has_prev:
        i += 1
    o_ref, s_out_ref, s_scr = refs[i:i + 3]
    t = pl.program_id(1)
    _, dk, dv = s_scr.shape

    @pl.when(t == 0)
    def _init():
        if zero_init:
            s_scr[...] = jnp.zeros_like(s_scr)
        else:
            s_scr[...] = s0_ref[...].reshape(s_scr.shape)

    ga = _dot(glr_ref[...].astype(BF16), wg_ref[...]) + bg_ref[...]
    log_a = _log_sigmoid(ga) * inv_tau
    tri = jnp.where(_tri(L), 1.0, 0.0).astype(BF16)
    tri2 = jnp.concatenate([tri, tri], axis=1)
    mask = _tri(L)
    ones2 = jnp.ones((2 * L, LANE), BF16)
    heads = range(H)
    ck = [slice(h * dk, (h + 1) * dk) for h in heads]
    cv = [slice(h * dv, (h + 1) * dv) for h in heads]
    for unit in range(ns * nc):
        sl = slice(unit * L, (unit + 1) * L)
        s0 = (unit // nc) * H
        g2 = _split_rows(log_a[sl, :])
        b = _dot(tri2, g2)
        tot = [_dot_tn(g2[:, ck[h]], ones2) for h in heads]
        eb = jnp.exp(b)
        q_in = (q_ref[sl, :] * scale * eb).astype(BF16)
        k = k_ref[sl, :]
        k_in = (k * jnp.exp(-b)).astype(BF16)
        k_out = (k * jnp.exp(b[L - 1:L, :] - b)).astype(BF16)
        v = [v_ref[sl, cv[h]].astype(BF16) for h in heads]
        att = [jnp.where(mask, _dot_nt(q_in[:, ck[h]], k_in[:, ck[h]]), 0.0).astype(BF16) for h in heads]
        s = [s_scr[s0 + h] for h in heads]
        o = [_dot(q_in[:, ck[h]], s[h].astype(BF16)) + _dot(att[h], v[h]) for h in heads]
        upd = [_dot_tn(k_out[:, ck[h]], v[h]) for h in heads]
        for h in heads:
            decay = jnp.exp(tot[h])
            s_scr[s0 + h] = s[h] * jnp.concatenate([decay] * (dv // LANE), axis=1) + upd[h]
            on = _rms(o[h]) * gn_ref[...]
            o_ref[sl, cv[h]] = (on * _silu(r_ref[sl, cv[h]])).astype(o_ref.dtype)

    @pl.when(t == nt - 1)
    def _fin():
        s_out_ref[...] = s_scr[...].reshape(s_out_ref.shape)


def _ssd_kernel(*refs, L, nc, nt, G, hpg, P, N, zero_init, has_prev, has_exl):
    (z_ref, x_ref, b_ref, c_ref, dt_ref, cw_ref, cb_ref, dtb_ref, alog_ref, dsk_ref, nrm_ref, exp_ref) = refs[:12]
    i = 12
    exl_ref = exp_ref
    if has_exl:
        exl_ref = refs[i]
        i += 1
    cs_ref = h0_ref = None
    if not zero_init:
        cs_ref, h0_ref = refs[i:i + 2]
        i += 2
    if has_prev:
        i += 1
    y_ref, h_out_ref, u, h_scr, xbd = refs[i:i + 5]
    t = pl.program_id(1)
    Tb = L * nc
    gw = hpg * P
    inner = G * gw
    wl = hpg * L
    nconv = cw_ref.shape[0]
    halo = SUBLANE

    @pl.when(t == 0)
    def _init():
        u[0:halo, :] = jnp.zeros((halo, u.shape[1]), F32)
        if zero_init:
            h_scr[...] = jnp.zeros_like(h_scr)
        else:
            u[halo - (nconv - 1):halo, :] = cs_ref[...]
            h_scr[...] = h0_ref[...].reshape(inner, N)
        xbd[...] = jnp.zeros_like(xbd)

    u[halo:halo + Tb, 0:inner] = x_ref[...]
    u[halo:halo + Tb, inner:inner + G * N] = b_ref[...]
    u[halo:halo + Tb, inner + G * N:inner + 2 * G * N] = c_ref[...]

    def conv(r0, lo, width):
        win = u[pl.ds(r0, L + halo), lo:lo + width]
        acc = cb_ref[:, lo:lo + width] + win[halo:, :] * cw_ref[nconv - 1:nconv, lo:lo + width]
        for j in range(nconv - 1):
            moved = pltpu.roll(win, nconv - 1 - j, 0)[halo:, :]
            acc = acc + moved * cw_ref[j:j + 1, lo:lo + width]
        return _silu(acc)

    dtp = _softplus(dt_ref[...] + dtb_ref[...])
    a_h = dtp * (-jnp.exp(alog_ref[...]))

    tri = jnp.where(_tri(L), 1.0, 0.0).astype(BF16)
    tri2 = jnp.concatenate([tri, tri], axis=1)
    ones2 = jnp.ones((2 * L, LANE), BF16)
    row_l = lax.broadcasted_iota(jnp.int32, (L, G * wl), 0)
    s_l = lax.broadcasted_iota(jnp.int32, (L, G * wl), 1) % L
    later = row_l > s_l
    causal = row_l >= s_l
    early_cast = L % 16 == 0
    lane_in_piece = lax.broadcasted_iota(jnp.int32, (L, LANE), 1)
    groups = range(G)
    gc = [slice(g * gw, (g + 1) * gw) for g in groups]

    for c in range(nc):
        sl = slice(c * L, (c + 1) * L)
        da2 = _split_lanes(jnp.concatenate([dtp[sl], a_h[sl]], axis=0))
        da_exp = _dot(da2, exp_ref[...])
        dt_exp = da_exp[:L]
        a_exp_p = da_exp[L:]
        a_exp_l = _dot(_split_lanes(a_h[sl]), exl_ref[...]) if has_exl else a_exp_p
        ap2 = _split_rows(a_exp_p)
        al2 = _split_rows(jnp.where(later, a_exp_l, 0.0))
        ccol = _dot(tri2, ap2)
        seg = _dot(tri2, al2)
        decay = jnp.where(causal, jnp.exp(seg), 0.0)
        grow = jnp.exp(ccol)
        fade = jnp.exp(ccol[L - 1:L, :] - ccol)

        xs, bs, w_cats, inters = [], [], [], []
        for g in groups:
            x = conv(c * L, g * gw, gw)
            b = conv(c * L, inner + g * N, N)
            cm = conv(c * L, inner + G * N + g * N, N).astype(BF16)
            if early_cast:
                b_t = jnp.concatenate([b.astype(BF16)] * hpg, axis=0)
            else:
                b_t = jnp.concatenate([b] * hpg, axis=0).astype(BF16)
            w_cats.append((_dot_nt(cm, b_t) * decay[:, g * wl:(g + 1) * wl]).astype(BF16))
            inters.append(_dot_nt(cm, h_scr[gc[g], :].astype(BF16)))
            xs.append(x)
            bs.append(b)

        xts, intras = [], []
        for g in groups:
            xt = xs[g] * dt_exp[:, gc[g]]
            xt_c = xt.astype(xbd.dtype)
            slab = g * nc + c
            for hh in range(hpg):
                lo = (hh * P) // LANE * LANE
                piece = xt_c[:, lo:lo + LANE]
                own = (lane_in_piece >= hh * P - lo) & (lane_in_piece < (hh + 1) * P - lo)
                xbd[slab, hh * L:(hh + 1) * L, lo:lo + LANE] = jnp.where(own, piece, jnp.zeros_like(piece))
            intras.append(_dot(w_cats[g], xbd[slab].astype(BF16)))
            xts.append(xt)

        for g in groups:
            x_dl = (xts[g] * fade[:, gc[g]]).astype(BF16)
            tot = _dot_tn(ap2[:, gc[g]], ones2)
            h = h_scr[gc[g], :]
            h_scr[gc[g], :] = (h * jnp.concatenate([jnp.exp(tot)] * (N // LANE), axis=1)
                               + _dot_tn(x_dl, bs[g].astype(BF16)))

        for g in groups:
            y = intras[g] + grow[:, gc[g]] * inters[g] + dsk_ref[:, gc[g]] * xs[g]
            yg = y * _silu(z_ref[sl, gc[g]])
            y_ref[sl, gc[g]] = (_rms(yg) * nrm_ref[:, gc[g]]).astype(y_ref.dtype)

    u[0:halo, :] = u[Tb:Tb + halo, :]

    @pl.when(t == nt - 1)
    def _fin():
        h_out_ref[...] = h_scr[...].reshape(h_out_ref.shape)


def _ada(c_all, w_ada, b_ada, tn=512):
    depth, d, n = w_ada.shape
    m = c_all.shape[0]
    return pl.pallas_call(
        _ada_kernel,
        grid=(depth, n // tn),
        in_specs=[
            pl.BlockSpec((m, d), lambda l, j: (0, 0)),
            pl.BlockSpec((None, d, tn), lambda l, j: (l, 0, j)),
            pl.BlockSpec((None, 1, tn), lambda l, j: (l, 0, j)),
        ],
        out_specs=pl.BlockSpec((None, m, tn), lambda l, j: (l, 0, j)),
        out_shape=jax.ShapeDtypeStruct((depth, m, n), F32),
        compiler_params=_cparams(("parallel", "parallel")),
        name="ada",
    )(c_all, w_ada, b_ada.reshape(depth, 1, n))


class _Rows:
    def __init__(self, n_seq, seq_len, long_seq, tm_max=1024):
        self.n_seq, self.seq_len, self.long_seq = n_seq, seq_len, long_seq
        self.m = n_seq * seq_len
        if long_seq:
            self.tm = min(tm_max, seq_len)
            assert seq_len % self.tm == 0
            self.reps = 1
            self.mod_rows = 1
        else:
            self.tm = min(self.m, tm_max)
            assert self.m % self.tm == 0 and self.tm % seq_len == 0
            self.reps = seq_len
            self.mod_rows = self.tm // seq_len
        self.n_tiles = self.m // self.tm

    def lhs_spec(self, width):
        if self.n_tiles == 1:
            return pl.BlockSpec((self.tm, width), lambda i, j: (i, 0), pipeline_mode=pl.Buffered(1))
        return pl.BlockSpec((self.tm, width), lambda i, j: (i, 0))

    def mod_array(self, mod):
        depth, _, n = mod.shape
        if self.reps == 1:
            return mod.reshape(depth, self.n_seq, 1, n)
        return mod

    def mod_spec(self, l, tn, col_block):
        if self.reps == 1:
            per = self.seq_len // self.tm
            return pl.BlockSpec((None, None, 1, tn), lambda i, j: (l, i // per, 0, col_block(j)))
        return pl.BlockSpec((None, self.mod_rows, tn), lambda i, j: (l, i, col_block(j)))


def _final_norm(x, gain, rows):
    m, d = x.shape
    return pl.pallas_call(
        _final_norm_kernel,
        grid=(rows.n_tiles,),
        in_specs=[pl.BlockSpec((rows.tm, d), lambda i: (i, 0)), pl.BlockSpec((1, d), lambda i: (0, 0))],
        out_specs=pl.BlockSpec((rows.tm, d), lambda i: (i, 0)),
        out_shape=jax.ShapeDtypeStruct((m, d), F32),
        compiler_params=_cparams(("parallel",)),
        name="final_norm",
    )(x, gain.reshape(1, d))


def _in_proj(x, mod_arr, w_t, l, rows, tn, v_scale, v_shift):
    m, d = x.shape
    n = w_t.shape[1]
    return pl.pallas_call(
        functools.partial(_norm_proj_kernel, reps=rows.reps),
        grid=(rows.n_tiles, n // tn),
        in_specs=[
            rows.lhs_spec(d),
            rows.mod_spec(l, d, lambda j: v_scale),
            rows.mod_spec(l, d, lambda j: v_shift),
            pl.BlockSpec((None, tn, d), lambda i, j: (l, j, 0)),
        ],
        out_specs=pl.BlockSpec((rows.tm, tn), lambda i, j: (i, j)),
        out_shape=jax.ShapeDtypeStruct((m, n), F32),
        scratch_shapes=[pltpu.VMEM((rows.tm, d), BF16)],
        compiler_params=_cparams(("parallel", "arbitrary")),
        name="in_proj",
    )(x, mod_arr, mod_arr, w_t)


def _merge(cfg, oa, yb, proj, wa, wb, l, rows):
    m = oa.shape[0]
    d = cfg.d_model
    tn = 512 if oa.dtype == BF16 else 256
    ga0 = cfg.off_gates // tn
    gb0 = (cfg.off_gates + d) // tn
    return pl.pallas_call(
        _merge_kernel,
        grid=(rows.n_tiles, d // tn),
        in_specs=[
            rows.lhs_spec(oa.shape[1]),
            rows.lhs_spec(yb.shape[1]),
            pl.BlockSpec((None, wa.shape[1], tn), lambda i, j: (l, 0, j)),
            pl.BlockSpec((None, wb.shape[1], tn), lambda i, j: (l, 0, j)),
            pl.BlockSpec((rows.tm, tn), lambda i, j: (i, ga0 + j)),
            pl.BlockSpec((rows.tm, tn), lambda i, j: (i, gb0 + j)),
        ],
        out_specs=pl.BlockSpec((rows.tm, tn), lambda i, j: (i, j)),
        out_shape=jax.ShapeDtypeStruct((m, d), BF16),
        compiler_params=_cparams(("parallel", "arbitrary")),
        name="merge",
    )(oa, yb, wa, wb, proj, proj)


def _resid(a, w, x, mod_arr, rows, l, v_gate, tn=512, name="resid"):
    m, kdim = a.shape
    d = x.shape[1]
    per = d // tn
    return pl.pallas_call(
        functools.partial(_resid_kernel, reps=rows.reps),
        grid=(rows.n_tiles, d // tn),
        in_specs=[
            rows.lhs_spec(kdim),
            pl.BlockSpec((None, kdim, tn), lambda i, j: (l, 0, j)),
            pl.BlockSpec((rows.tm, tn), lambda i, j: (i, j)),
            rows.mod_spec(l, tn, lambda j: v_gate * per + j),
        ],
        out_specs=pl.BlockSpec((rows.tm, tn), lambda i, j: (i, j)),
        out_shape=jax.ShapeDtypeStruct((m, d), F32),
        compiler_params=_cparams(("parallel", "arbitrary")),
        name=name,
    )(a, w, x, mod_arr)


def _ffn_in(x, mod_arr, w, l, rows, d_ff, v_scale, v_shift, tn=512):
    m, d = x.shape
    nb = d_ff // tn
    return pl.pallas_call(
        functools.partial(_norm_ffn_in_kernel, reps=rows.reps),
        grid=(rows.n_tiles, nb),
        in_specs=[
            rows.lhs_spec(d),
            rows.mod_spec(l, d, lambda j: v_scale),
            rows.mod_spec(l, d, lambda j: v_shift),
            pl.BlockSpec((None, d, tn), lambda i, j: (l, 0, j)),
            pl.BlockSpec((None, d, tn), lambda i, j: (l, 0, nb + j)),
        ],
        out_specs=pl.BlockSpec((rows.tm, tn), lambda i, j: (i, j)),
        out_shape=jax.ShapeDtypeStruct((m, d_ff), BF16),
        scratch_shapes=[pltpu.VMEM((rows.tm, d), BF16)],
        compiler_params=_cparams(("parallel", "arbitrary")),
        name="ffn_in",
    )(x, mod_arr, mod_arr, w, w)


def _scan_blocks(rows, chunk, max_chunks):
    L = min(chunk, rows.seq_len)
    assert rows.seq_len % L == 0
    n_chunks = rows.seq_len // L
    nc = min(max_chunks, n_chunks)
    assert n_chunks % nc == 0
    return L, nc, n_chunks // nc


def _gla(cfg, proj, l, rows, wg, bg, gn, state, prev, out_dtype):
    H, dk, dv = cfg.gla_heads, cfg.dk, cfg.dv
    hk, hv = H * dk, H * dv
    B = rows.n_seq
    L, nc, nt = _scan_blocks(rows, cfg.gla_chunk, 4)
    ns = 2 if (nt == 1 and nc == 1 and B % 2 == 0) else 1
    Tb = ns * L * nc
    assert cfg.off_q % hk == 0 and cfg.off_v % hv == 0
    rb = lambda b, t: b * nt + t
    in_specs = [
        pl.BlockSpec((Tb, hk), lambda b, t: (rb(b, t), cfg.off_q // hk)),
        pl.BlockSpec((Tb, hk), lambda b, t: (rb(b, t), cfg.off_k // hk)),
        pl.BlockSpec((Tb, hv), lambda b, t: (rb(b, t), cfg.off_v // hv)),
        pl.BlockSpec((Tb, LANE), lambda b, t: (rb(b, t), cfg.off_glr // LANE)),
        pl.BlockSpec((Tb, hv), lambda b, t: (rb(b, t), cfg.off_r // hv)),
        pl.BlockSpec((None, LANE, hk), lambda b, t: (l, 0, 0)),
        pl.BlockSpec((None, 1, hk), lambda b, t: (l, 0, 0)),
        pl.BlockSpec((None, 1, dv), lambda b, t: (l, 0, 0)),
    ]
    args = [proj, proj, proj, proj, proj, wg, bg, gn]
    state_spec = pl.BlockSpec((None, ns, H, dk, dv), lambda b, t: (l, b, 0, 0, 0))
    if state is not None:
        in_specs.append(state_spec)
        args.append(state)
    aliases = {}
    if prev is not None:
        in_specs.append(pl.BlockSpec(memory_space=pl.ANY))
        aliases = {len(args): 1}
        args.append(prev)
    kern = functools.partial(_gla_kernel, L=L, nc=nc, nt=nt, ns=ns, H=H, zero_init=state is None,
                             has_prev=prev is not None, scale=dk ** -0.5, inv_tau=1.0 / cfg.gla_tau)
    return pl.pallas_call(
        kern,
        grid=(B // ns, nt),
        in_specs=in_specs,
        out_specs=[pl.BlockSpec((Tb, hv), lambda b, t: (rb(b, t), 0)), state_spec],
        out_shape=[
            jax.ShapeDtypeStruct((rows.m, hv), out_dtype),
            jax.ShapeDtypeStruct((cfg.depth, B, H, dk, dv), F32),
        ],
        scratch_shapes=[pltpu.VMEM((ns * H, dk, dv), F32)],
        input_output_aliases=aliases,
        compiler_params=_cparams(("parallel", "arbitrary")),
        name="gla_scan",
    )(*args)


def _head_expander(n_heads, per):
    col_head = jnp.arange(n_heads * per, dtype=jnp.int32) // per
    ex = (col_head[None, :] == jnp.arange(LANE, dtype=jnp.int32)[:, None]).astype(BF16)
    return jnp.concatenate([ex, ex], axis=0)


def _ssd(cfg, proj, l, rows, conv_w, conv_b, dtb, alog, dsk, nrm, conv_state, state, prev, out_dtype):
    G, gw, N, P, hpg = cfg.ssd_groups, cfg.gw, cfg.ssd_state, cfg.ssd_head_dim, cfg.hpg
    inner, heads, cdim = cfg.inner, cfg.ssd_heads, cfg.conv_dim
    B = rows.n_seq
    K = cfg.ssd_conv
    L, nc, nt = _scan_blocks(rows, cfg.ssd_chunk, 2)
    Tb = L * nc
    wl = hpg * L
    gn = G * N
    assert cfg.off_z % inner == 0 and cfg.off_xbc % inner == 0 and (cfg.off_xbc + inner) % gn == 0
    rb = lambda b, t: b * nt + t
    in_specs = [
        pl.BlockSpec((Tb, inner), lambda b, t: (rb(b, t), cfg.off_z // inner)),
        pl.BlockSpec((Tb, inner), lambda b, t: (rb(b, t), cfg.off_xbc // inner)),
        pl.BlockSpec((Tb, gn), lambda b, t: (rb(b, t), (cfg.off_xbc + inner) // gn)),
        pl.BlockSpec((Tb, gn), lambda b, t: (rb(b, t), (cfg.off_xbc + inner) // gn + 1)),
        pl.BlockSpec((Tb, LANE), lambda b, t: (rb(b, t), cfg.off_dt // LANE)),
        pl.BlockSpec((None, K, cdim), lambda b, t: (l, 0, 0)),
        pl.BlockSpec((None, 1, cdim), lambda b, t: (l, 0, 0)),
        pl.BlockSpec((None, 1, LANE), lambda b, t: (l, 0, 0)),
        pl.BlockSpec((None, 1, LANE), lambda b, t: (l, 0, 0)),
        pl.BlockSpec((None, 1, inner), lambda b, t: (l, 0, 0)),
        pl.BlockSpec((None, 1, inner), lambda b, t: (l, 0, 0)),
        pl.BlockSpec((2 * LANE, inner), lambda b, t: (0, 0)),
    ]
    args = [proj, proj, proj, proj, proj, conv_w, conv_b, dtb, alog, dsk, nrm, _head_expander(heads, P)]
    has_exl = L != P
    if has_exl:
        in_specs.append(pl.BlockSpec((2 * LANE, heads * L), lambda b, t: (0, 0)))
        args.append(_head_expander(heads, L))
    state_spec = pl.BlockSpec((None, None, heads, P, N), lambda b, t: (l, b, 0, 0, 0))
    if state is not None:
        in_specs += [pl.BlockSpec((None, None, K - 1, cdim), lambda b, t: (l, b, 0, 0)), state_spec]
        args += [conv_state, state]
    aliases = {}
    if prev is not None:
        in_specs.append(pl.BlockSpec(memory_space=pl.ANY))
        aliases = {len(args): 1}
        args.append(prev)
    kern = functools.partial(_ssd_kernel, L=L, nc=nc, nt=nt, G=G, hpg=hpg, P=P, N=N, zero_init=state is None,
                             has_prev=prev is not None, has_exl=has_exl)
    xbd_dtype = BF16 if L % 16 == 0 else F32
    return pl.pallas_call(
        kern,
        grid=(B, nt),
        in_specs=in_specs,
        out_specs=[pl.BlockSpec((Tb, inner), lambda b, t: (rb(b, t), 0)), state_spec],
        out_shape=[
            jax.ShapeDtypeStruct((rows.m, inner), out_dtype),
            jax.ShapeDtypeStruct((cfg.depth, B, heads, P, N), F32),
        ],
        scratch_shapes=[
            pltpu.VMEM((Tb + SUBLANE, cdim), F32),
            pltpu.VMEM((inner, N), F32),
            pltpu.VMEM((G * nc, wl, gw), xbd_dtype),
        ],
        input_output_aliases=aliases,
        compiler_params=_cparams(("parallel", "arbitrary")),
        name="ssd_scan",
    )(*args)


def _pack_w_in(cfg, w_in, tn):
    H, dk, dv = cfg.gla_heads, cfg.dk, cfg.dv
    sizes = (H * dk, H * dk, H * dv, H * dv, cfg.gla_rank, cfg.inner, cfg.conv_dim, cfg.ssd_heads, 2 * cfg.d_model)
    offs = [0]
    for s in sizes:
        offs.append(offs[-1] + s)
    w_t = jnp.swapaxes(w_in, 1, 2)
    seg = lambda i: w_t[:, offs[i]:offs[i + 1], :].astype(BF16)
    padto = lambda a, n: jnp.pad(a, ((0, 0), (0, n - a.shape[1]), (0, 0)))
    n_p = cfg.n_packed(tn)
    parts = [seg(5), seg(6), seg(0), seg(1), seg(2), seg(3), seg(8), padto(seg(4), LANE), padto(seg(7), LANE)]
    return padto(jnp.concatenate(parts, axis=1), n_p)


def _trunk(cfg, x, mod, rows, states, weights, scan_dtype):
    (w_in_p, wg, bg, gn, w_gla_proj, conv_w, conv_b, dtb, alog, dsk, nrm, w_ssd_proj, w_mix_out,
     w_ffn_in, w_ffn_out, final_norm, tn_in) = weights
    s_gla, s_ssm, s_conv = states
    mod_arr = rows.mod_array(mod)
    rows_e = _Rows(rows.n_seq, rows.seq_len, rows.long_seq, tm_max=256)
    B = rows.n_seq
    K = cfg.ssd_conv
    new_gla = new_ssm = None
    new_conv = []
    for l in range(cfg.depth):
        proj = _in_proj(x, mod_arr, w_in_p, l, rows, tn_in, 1, 0)
        oa, new_gla = _gla(cfg, proj, l, rows, wg, bg, gn, s_gla, new_gla, scan_dtype)
        yb, new_ssm = _ssd(cfg, proj, l, rows, conv_w, conv_b, dtb, alog, dsk, nrm, s_conv, s_ssm, new_ssm,
                           scan_dtype)
        assert rows.seq_len >= K - 1
        tail = proj.reshape(B, rows.seq_len, proj.shape[1])[:, rows.seq_len - (K - 1):, :]
        new_conv.append(tail[:, :, cfg.off_xbc:cfg.off_xbc + cfg.conv_dim])
        merged = _merge(cfg, oa, yb, proj, w_gla_proj, w_ssd_proj, l, rows)
        x = _resid(merged, w_mix_out, x, mod_arr, rows, l, 2, name="mix_out")
        act = _ffn_in(x, mod_arr, w_ffn_in, l, rows, cfg.d_ff, 4, 3)
        x = _resid(act, w_ffn_out, x, mod_arr, rows, l, 5, name="ffn_out")
    y = _final_norm(x, final_norm, rows_e)
    return y, new_gla, new_ssm, jnp.stack(new_conv)


def _forward(cfg, x_prompt, x_sample, c_prompt, c_sample, state_gla, state_ssm, state_conv,
             w_ada, b_ada, w_in, w_gla_gate, b_gla_gate, gla_norm, w_gla_proj, conv_w, conv_b,
             dt_bias, A_log, d_skip, ssd_norm, w_ssd_proj, w_mix_out, w_ffn_in, w_ffn_out, final_norm):
    depth, d = cfg.depth, cfg.d_model
    bp, tp, _ = x_prompt.shape
    bs, ts, _ = x_sample.shape
    H, dk, dv = cfg.gla_heads, cfg.dk, cfg.dv
    P = cfg.ssd_head_dim
    assert cfg.ssd_heads <= LANE and cfg.gla_rank <= LANE
    assert cfg.off_gates % 256 == 0

    pad = (-bp) % SUBLANE
    c_all = jnp.concatenate([c_prompt, jnp.zeros((pad, d), F32), c_sample], axis=0)
    mod = _ada(c_all, w_ada, b_ada)
    mod_p = mod[:, :bp]
    mod_s = mod[:, bp + pad:]

    tn_in = 768
    padl = lambda a: jnp.pad(a, ((0, 0), (0, LANE - a.shape[1])))
    weights = (
        _pack_w_in(cfg, w_in, tn_in),
        jnp.pad(w_gla_gate, ((0, 0), (0, LANE - cfg.gla_rank), (0, 0))).astype(BF16),
        b_gla_gate.reshape(depth, 1, H * dk),
        gla_norm.reshape(depth, 1, dv),
        w_gla_proj.astype(BF16),
        conv_w,
        conv_b.reshape(depth, 1, cfg.conv_dim),
        padl(dt_bias).reshape(depth, 1, LANE),
        padl(A_log).reshape(depth, 1, LANE),
        jnp.repeat(d_skip, P, axis=1).reshape(depth, 1, cfg.inner),
        ssd_norm.reshape(depth, 1, cfg.inner),
        w_ssd_proj.astype(BF16),
        w_mix_out.astype(BF16),
        w_ffn_in.astype(BF16),
        w_ffn_out.astype(BF16),
        final_norm,
        tn_in,
    )
    rows_p = _Rows(bp, tp, True)
    rows_s = _Rows(bs, ts, False)
    y_p, gla_p, ssm_p, conv_p = _trunk(cfg, x_prompt.reshape(bp * tp, d), mod_p, rows_p,
                                       (None, None, None), weights, BF16)
    states_s = (state_gla, state_ssm, state_conv)
    y_s, gla_s, ssm_s, conv_s = _trunk(cfg, x_sample.reshape(bs * ts, d), mod_s, rows_s,
                                       states_s, weights, F32)
    return (y_p.reshape(bp, tp, d), y_s.reshape(bs, ts, d), gla_p, ssm_p, conv_p, gla_s, ssm_s, conv_s)


def kernel(x_prompt, x_sample, c_prompt, c_sample, state_gla, state_ssm, state_conv, w_ada, b_ada, w_in,
           w_gla_gate, b_gla_gate, gla_norm, w_gla_proj, conv_w, conv_b, dt_bias, A_log, d_skip, ssd_norm,
           w_ssd_proj, w_mix_out, w_ffn_in, w_ffn_out, final_norm):
    cfg = Cfg()
    assert x_prompt.shape[2] == cfg.d_model
    return _forward(cfg, x_prompt, x_sample, c_prompt, c_sample, state_gla, state_ssm, state_conv,
                    w_ada, b_ada, w_in, w_gla_gate, b_gla_gate, gla_norm, w_gla_proj, conv_w, conv_b,
                    dt_bias, A_log, d_skip, ssd_norm, w_ssd_proj, w_mix_out, w_ffn_in, w_ffn_out, final_norm)
```

```python
import dataclasses
import functools

import jax
import jax.numpy as jnp
from jax import lax
from jax.experimental import pallas as pl
from jax.experimental.pallas import tpu as pltpu

F32 = jnp.float32
BF16 = jnp.bfloat16

LANE = 128
SUBLANE = 8
VMEM_LIMIT = 56 * 1024 * 1024
EPS = 1e-6
N_ADA = 6


@dataclasses.dataclass(frozen=True)
class Cfg:
    d_model: int = 2048
    depth: int = 2
    gla_heads: int = 4
    gla_rank: int = 16
    gla_tau: float = 16.0
    gla_chunk: int = 64
    ssd_head_dim: int = 64
    ssd_groups: int = 8
    ssd_state: int = 128
    ssd_conv: int = 4
    ssd_chunk: int = 64

    @property
    def dk(self):
        return self.d_model // 2 // self.gla_heads

    @property
    def dv(self):
        return self.d_model // self.gla_heads

    @property
    def inner(self):
        return 2 * self.d_model

    @property
    def ssd_heads(self):
        return self.inner // self.ssd_head_dim

    @property
    def hpg(self):
        return self.ssd_heads // self.ssd_groups

    @property
    def gw(self):
        return self.inner // self.ssd_groups

    @property
    def conv_dim(self):
        return self.inner + 2 * self.ssd_groups * self.ssd_state

    @property
    def d_ff(self):
        return ((8 * self.d_model // 3 + 255) // 256) * 256

    @property
    def off_z(self):
        return 0

    @property
    def off_xbc(self):
        return self.inner

    @property
    def off_q(self):
        return self.off_xbc + self.conv_dim

    @property
    def off_k(self):
        return self.off_q + self.gla_heads * self.dk

    @property
    def off_v(self):
        return self.off_k + self.gla_heads * self.dk

    @property
    def off_r(self):
        return self.off_v + self.gla_heads * self.dv

    @property
    def off_gates(self):
        return self.off_r + self.gla_heads * self.dv

    @property
    def off_glr(self):
        return self.off_gates + 2 * self.d_model

    @property
    def off_dt(self):
        return self.off_glr + LANE

    def n_packed(self, tn):
        n = self.off_dt + LANE
        return -(-n // tn) * tn


def _cparams(sem):
    return pltpu.CompilerParams(dimension_semantics=sem, vmem_limit_bytes=VMEM_LIMIT)


def _sigmoid(x):
    return 1.0 / (1.0 + jnp.exp(-x))


def _silu(x):
    hx = 0.5 * x
    return hx + hx * jnp.tanh(hx)


def _softplus(x):
    return jnp.maximum(x, 0.0) + jnp.log1p(jnp.exp(-jnp.abs(x)))


def _log_sigmoid(x):
    return jnp.minimum(x, 0.0) - jnp.log(1.0 + jnp.exp(-jnp.abs(x)))


def _dot(a, b):
    return jnp.dot(a, b, preferred_element_type=F32)


def _dot_nt(a, b):
    return lax.dot_general(a, b, (((1,), (1,)), ((), ())), preferred_element_type=F32)


def _dot_tn(a, b):
    return lax.dot_general(a, b, (((0,), (0,)), ((), ())), preferred_element_type=F32)


def _split(x, parts):
    out = []
    r = x
    for i in range(parts):
        p = r.astype(BF16)
        out.append(p)
        if i + 1 < parts:
            r = r - p.astype(F32)
    return out


def _split_rows(x):
    hi = x.astype(BF16).astype(F32)
    return jnp.concatenate([hi, x - hi], axis=0).astype(BF16)


def _split_lanes(x):
    hi = x.astype(BF16)
    return jnp.concatenate([hi, (x - hi.astype(F32)).astype(BF16)], axis=1)


def _rms(x):
    return x * lax.rsqrt(jnp.mean(x * x, axis=-1, keepdims=True) + EPS)


def _expand_rows(v, reps):
    if reps == 1:
        return v
    g = v.shape[0]
    r = lax.broadcasted_iota(jnp.int32, (g * reps, g), 0)
    c = lax.broadcasted_iota(jnp.int32, (g * reps, g), 1)
    lo = c * reps
    e = jnp.where((r >= lo) & (r < lo + reps), 1.0, 0.0).astype(BF16)
    acc = None
    for p in _split(v, 3):
        d = _dot(e, p)
        acc = d if acc is None else acc + d
    return acc


def _tri(n, strict=False, upper=False):
    r = lax.broadcasted_iota(jnp.int32, (n, n), 0)
    c = lax.broadcasted_iota(jnp.int32, (n, n), 1)
    if upper:
        r, c = c, r
    return (r > c) if strict else (r >= c)


def _ada_kernel(c_ref, w_ref, b_ref, o_ref):
    a = _silu(c_ref[...]).astype(BF16)
    o_ref[...] = _dot(a, w_ref[...].astype(BF16)) + b_ref[...]


NORM_ROWS = 256


def _modulated_norm(x_ref, sc_ref, sh_ref, h_scr, reps):
    @pl.when(pl.program_id(1) == 0)
    def _():
        tm = x_ref.shape[0]
        step = min(NORM_ROWS, tm)
        for r in range(0, tm, step):
            rs = slice(r, r + step)
            if reps == 1:
                sc, sh = sc_ref[...], sh_ref[...]
            else:
                ms = slice(r // reps, (r + step) // reps)
                sc = _expand_rows(sc_ref[ms, :], reps)
                sh = _expand_rows(sh_ref[ms, :], reps)
            h_scr[rs, :] = (_rms(x_ref[rs, :]) * (1.0 + sc) + sh).astype(h_scr.dtype)


def _norm_proj_kernel(x_ref, sc_ref, sh_ref, w_ref, o_ref, h_scr, *, reps):
    _modulated_norm(x_ref, sc_ref, sh_ref, h_scr, reps)
    o_ref[...] = _dot_nt(h_scr[...], w_ref[...])


def _norm_ffn_in_kernel(x_ref, sc_ref, sh_ref, wg_ref, wu_ref, o_ref, h_scr, *, reps):
    _modulated_norm(x_ref, sc_ref, sh_ref, h_scr, reps)
    h = h_scr[...]
    o_ref[...] = (_silu(_dot(h, wg_ref[...])) * _dot(h, wu_ref[...])).astype(o_ref.dtype)


def _final_norm_kernel(x_ref, g_ref, o_ref):
    o_ref[...] = _rms(x_ref[...]) * g_ref[...]


def _merge_kernel(oa_ref, yb_ref, wa_ref, wb_ref, ga_ref, gb_ref, o_ref):
    ya = _dot(oa_ref[...].astype(BF16), wa_ref[...])
    yb = _dot(yb_ref[...].astype(BF16), wb_ref[...])
    o_ref[...] = (_sigmoid(ga_ref[...]) * ya + _sigmoid(gb_ref[...]) * yb).astype(o_ref.dtype)


def _resid_kernel(a_ref, w_ref, x_ref, g_ref, o_ref, *, reps):
    y = _dot(a_ref[...], w_ref[...])
    o_ref[...] = x_ref[...] + _expand_rows(g_ref[...], reps) * y


def _gla_kernel(*refs, L, nc, nt, ns, H, zero_init, has_prev, scale, inv_tau):
    q_ref, k_ref, v_ref, glr_ref, r_ref, wg_ref, bg_ref, gn_ref = refs[:8]
    i = 8
    s0_ref = None
    if not zero_init:
        s0_ref = refs[i]
        i += 1
    if has_prev:
        i += 1
    o_ref, s_out_ref, s_scr = refs[i:i + 3]
    t = pl.program_id(1)
    _, dk, dv = s_scr.shape

    @pl.when(t == 0)
    def _init():
        if zero_init:
            s_scr[...] = jnp.zeros_like(s_scr)
        else:
            s_scr[...] = s0_ref[...].reshape(s_scr.shape)

    ga = _dot(glr_ref[...].astype(BF16), wg_ref[...]) + bg_ref[...]
    log_a = _log_sigmoid(ga) * inv_tau
    tri = jnp.where(_tri(L), 1.0, 0.0).astype(BF16)
    tri2 = jnp.concatenate([tri, tri], axis=1)
    mask = _tri(L)
    ones2 = jnp.ones((2 * L, LANE), BF16)
    heads = range(H)
    ck = [slice(h * dk, (h + 1) * dk) for h in heads]
    cv = [slice(h * dv, (h + 1) * dv) for h in heads]
    for unit in range(ns * nc):
        sl = slice(unit * L, (unit + 1) * L)
        s0 = (unit // nc) * H
        g2 = _split_rows(log_a[sl, :])
        b = _dot(tri2, g2)
        tot = [_dot_tn(g2[:, ck[h]], ones2) for h in heads]
        eb = jnp.exp(b)
        q_in = (q_ref[sl, :] * scale * eb).astype(BF16)
        k = k_ref[sl, :]
        k_in = (k * jnp.exp(-b)).astype(BF16)
        k_out = (k * jnp.exp(b[L - 1:L, :] - b)).astype(BF16)
        v = [v_ref[sl, cv[h]].astype(BF16) for h in heads]
        att = [jnp.where(mask, _dot_nt(q_in[:, ck[h]], k_in[:, ck[h]]), 0.0).astype(BF16) for h in heads]
        s = [s_scr[s0 + h] for h in heads]
        o = [_dot(q_in[:, ck[h]], s[h].astype(BF16)) + _dot(att[h], v[h]) for h in heads]
        upd = [_dot_tn(k_out[:, ck[h]], v[h]) for h in heads]
        for h in heads:
            decay = jnp.exp(tot[h])
            s_scr[s0 + h] = s[h] * jnp.concatenate([decay] * (dv // LANE), axis=1) + upd[h]
            on = _rms(o[h]) * gn_ref[...]
            o_ref[sl, cv[h]] = (on * _silu(r_ref[sl, cv[h]])).astype(o_ref.dtype)

    @pl.when(t == nt - 1)
    def _fin():
        s_out_ref[...] = s_scr[...].reshape(s_out_ref.shape)


def _ssd_kernel(*refs, L, nc, nt, G, hpg, P, N, zero_init, has_prev, has_exl):
    (z_ref, x_ref, b_ref, c_ref, dt_ref, cw_ref, cb_ref, dtb_ref, alog_ref, dsk_ref, nrm_ref, exp_ref) = refs[:12]
    i = 12
    exl_ref = exp_ref
    if has_exl:
        exl_ref = refs[i]
        i += 1
    cs_ref = h0_ref = None
    if not zero_init:
        cs_ref, h0_ref = refs[i:i + 2]
        i += 2
    if has_prev:
        i += 1
    y_ref, h_out_ref, u, h_scr, xbd = refs[i:i + 5]
    t = pl.program_id(1)
    Tb = L * nc
    gw = hpg * P
    inner = G * gw
    wl = hpg * L
    nconv = cw_ref.shape[0]
    halo = SUBLANE

    direct = nt == 1 and nc == 1

    @pl.when(t == 0)
    def _init():
        xbd[...] = jnp.zeros_like(xbd)
        u[0:halo, :] = jnp.zeros((halo, u.shape[1]), F32)
        if not zero_init:
            u[halo - (nconv - 1):halo, :] = cs_ref[...]
        if not direct:
            if zero_init:
                h_scr[...] = jnp.zeros_like(h_scr)
            else:
                h_scr[...] = h0_ref[...].reshape(inner, N)

    def load_h(g):
        if not direct:
            return h_scr[g * gw:(g + 1) * gw, :]
        if zero_init:
            return jnp.zeros((gw, N), F32)
        return h0_ref[g * hpg:(g + 1) * hpg].reshape(gw, N)

    def store_h(g, val):
        if direct:
            h_out_ref[g * hpg:(g + 1) * hpg] = val.reshape(hpg, P, N)
        else:
            h_scr[g * gw:(g + 1) * gw, :] = val

    u[halo:halo + Tb, 0:inner] = x_ref[...]
    u[halo:halo + Tb, inner:inner + G * N] = b_ref[...]
    u[halo:halo + Tb, inner + G * N:inner + 2 * G * N] = c_ref[...]

    def conv(r0, lo, width):
        win = u[pl.ds(r0, L + halo), lo:lo + width]
        acc = cb_ref[:, lo:lo + width] + win[halo:, :] * cw_ref[nconv - 1:nconv, lo:lo + width]
        for j in range(nconv - 1):
            moved = pltpu.roll(win, nconv - 1 - j, 0)[halo:, :]
            acc = acc + moved * cw_ref[j:j + 1, lo:lo + width]
        return _silu(acc)

    dtp = _softplus(dt_ref[...] + dtb_ref[...])
    a_h = dtp * (-jnp.exp(alog_ref[...]))

    tri = jnp.where(_tri(L), 1.0, 0.0).astype(BF16)
    tri2 = jnp.concatenate([tri, tri], axis=1)
    ones2 = jnp.ones((2 * L, LANE), BF16)
    row_l = lax.broadcasted_iota(jnp.int32, (L, G * wl), 0)
    s_l = lax.broadcasted_iota(jnp.int32, (L, G * wl), 1) % L
    later = row_l > s_l
    causal = row_l >= s_l
    early_cast = L % 16 == 0
    lane_in_piece = lax.broadcasted_iota(jnp.int32, (L, LANE), 1)
    gc = [slice(g * gw, (g + 1) * gw) for g in range(G)]
    gbatch = 2 if L >= P else G
    assert G % gbatch == 0

    for c in range(nc):
        sl = slice(c * L, (c + 1) * L)
        da2 = _split_lanes(jnp.concatenate([dtp[sl], a_h[sl]], axis=0))
        da_exp = _dot(da2, exp_ref[...])
        dt_exp = da_exp[:L]
        a_exp_p = da_exp[L:]
        a_exp_l = _dot(_split_lanes(a_h[sl]), exl_ref[...]) if has_exl else a_exp_p
        ap2 = _split_rows(a_exp_p)
        al2 = _split_rows(jnp.where(later, a_exp_l, 0.0))
        ccol = _dot(tri2, ap2)
        seg = _dot(tri2, al2)
        decay = jnp.where(causal, jnp.exp(seg), 0.0)
        grow = jnp.exp(ccol)
        fade = jnp.exp(ccol[L - 1:L, :] - ccol)
        lp = gc
        ll = [slice(g * wl, (g + 1) * wl) for g in range(G)]

        for g0 in range(0, G, gbatch):
            batch = range(g0, g0 + gbatch)
            xs, bs, w_cats, inters = {}, {}, {}, {}
            for g in batch:
                x = conv(c * L, g * gw, gw)
                b = conv(c * L, inner + g * N, N)
                cm = conv(c * L, inner + G * N + g * N, N).astype(BF16)
                if early_cast:
                    b_t = jnp.concatenate([b.astype(BF16)] * hpg, axis=0)
                else:
                    b_t = jnp.concatenate([b] * hpg, axis=0).astype(BF16)
                w_cats[g] = (_dot_nt(cm, b_t) * decay[:, ll[g]]).astype(BF16)
                inters[g] = _dot_nt(cm, load_h(g).astype(BF16))
                xs[g] = x
                bs[g] = b

            xts, intras = {}, {}
            for g in batch:
                xt = xs[g] * dt_exp[:, lp[g]]
                xt_c = xt.astype(xbd.dtype)
                slab = g * nc + c
                for hh in range(hpg):
                    lo = (hh * P) // LANE * LANE
                    piece = xt_c[:, lo:lo + LANE]
                    own = (lane_in_piece >= hh * P - lo) & (lane_in_piece < (hh + 1) * P - lo)
                    xbd[slab, hh * L:(hh + 1) * L, lo:lo + LANE] = jnp.where(own, piece, jnp.zeros_like(piece))
                intras[g] = _dot(w_cats[g], xbd[slab].astype(BF16))
                xts[g] = xt

            for g in batch:
                x_dl = (xts[g] * fade[:, lp[g]]).astype(BF16)
                tot = _dot_tn(ap2[:, lp[g]], ones2)
                h = load_h(g)
                store_h(g, h * jnp.concatenate([jnp.exp(tot)] * (N // LANE), axis=1)
                        + _dot_tn(x_dl, bs[g].astype(BF16)))

            for g in batch:
                y = intras[g] + grow[:, lp[g]] * inters[g] + dsk_ref[:, gc[g]] * xs[g]
                yg = y * _silu(z_ref[sl, gc[g]])
                y_ref[sl, gc[g]] = (_rms(yg) * nrm_ref[:, gc[g]]).astype(y_ref.dtype)

    u[0:halo, :] = u[Tb:Tb + halo, :]

    if not direct:
        @pl.when(t == nt - 1)
        def _fin():
            h_out_ref[...] = h_scr[...].reshape(h_out_ref.shape)


def _ada(c_all, w_ada, b_ada, tn=1024):
    depth, d, n = w_ada.shape
    m = c_all.shape[0]
    return pl.pallas_call(
        _ada_kernel,
        grid=(depth, n // tn),
        in_specs=[
            pl.BlockSpec((m, d), lambda l, j: (0, 0)),
            pl.BlockSpec((None, d, tn), lambda l, j: (l, 0, j)),
            pl.BlockSpec((None, 1, tn), lambda l, j: (l, 0, j)),
        ],
        out_specs=pl.BlockSpec((None, m, tn), lambda l, j: (l, 0, j)),
        out_shape=jax.ShapeDtypeStruct((depth, m, n), F32),
        compiler_params=_cparams(("parallel", "parallel")),
        name="ada",
    )(c_all, w_ada, b_ada.reshape(depth, 1, n))


class _Rows:
    def __init__(self, n_seq, seq_len, long_seq, tm_max=1024):
        self.n_seq, self.seq_len, self.long_seq = n_seq, seq_len, long_seq
        self.m = n_seq * seq_len
        if long_seq:
            self.tm = min(tm_max, seq_len)
            assert seq_len % self.tm == 0
            self.reps = 1
            self.mod_rows = 1
        else:
            self.tm = min(self.m, tm_max)
            assert self.m % self.tm == 0 and self.tm % seq_len == 0
            self.reps = seq_len
            self.mod_rows = self.tm // seq_len
        self.n_tiles = self.m // self.tm

    def lhs_spec(self, width):
        if self.n_tiles == 1:
            return pl.BlockSpec((self.tm, width), lambda i, j: (i, 0), pipeline_mode=pl.Buffered(1))
        return pl.BlockSpec((self.tm, width), lambda i, j: (i, 0))

    def mod_array(self, mod):
        depth, _, n = mod.shape
        if self.reps == 1:
            return mod.reshape(depth, self.n_seq, 1, n)
        return mod

    def mod_spec(self, l, tn, col_block):
        if self.reps == 1:
            per = self.seq_len // self.tm
            return pl.BlockSpec((None, None, 1, tn), lambda i, j: (l, i // per, 0, col_block(j)))
        return pl.BlockSpec((None, self.mod_rows, tn), lambda i, j: (l, i, col_block(j)))


def _final_norm(x, gain, rows):
    m, d = x.shape
    return pl.pallas_call(
        _final_norm_kernel,
        grid=(rows.n_tiles,),
        in_specs=[pl.BlockSpec((rows.tm, d), lambda i: (i, 0)), pl.BlockSpec((1, d), lambda i: (0, 0))],
        out_specs=pl.BlockSpec((rows.tm, d), lambda i: (i, 0)),
        out_shape=jax.ShapeDtypeStruct((m, d), F32),
        compiler_params=_cparams(("parallel",)),
        name="final_norm",
    )(x, gain.reshape(1, d))


def _in_proj(x, mod_arr, w_t, l, rows, tn, v_scale, v_shift):
    m, d = x.shape
    n = w_t.shape[1]
    return pl.pallas_call(
        functools.partial(_norm_proj_kernel, reps=rows.reps),
        grid=(rows.n_tiles, n // tn),
        in_specs=[
            rows.lhs_spec(d),
            rows.mod_spec(l, d, lambda j: v_scale),
            rows.mod_spec(l, d, lambda j: v_shift),
            pl.BlockSpec((None, tn, d), lambda i, j: (l, j, 0)),
        ],
        out_specs=pl.BlockSpec((rows.tm, tn), lambda i, j: (i, j)),
        out_shape=jax.ShapeDtypeStruct((m, n), F32),
        scratch_shapes=[pltpu.VMEM((rows.tm, d), BF16)],
        compiler_params=_cparams(("parallel", "arbitrary")),
        name="in_proj",
    )(x, mod_arr, mod_arr, w_t)


def _merge(cfg, oa, yb, proj, wa, wb, l, rows):
    m = oa.shape[0]
    d = cfg.d_model
    tn = 512 if oa.dtype == BF16 else 256
    ga0 = cfg.off_gates // tn
    gb0 = (cfg.off_gates + d) // tn
    return pl.pallas_call(
        _merge_kernel,
        grid=(rows.n_tiles, d // tn),
        in_specs=[
            rows.lhs_spec(oa.shape[1]),
            rows.lhs_spec(yb.shape[1]),
            pl.BlockSpec((None, wa.shape[1], tn), lambda i, j: (l, 0, j)),
            pl.BlockSpec((None, wb.shape[1], tn), lambda i, j: (l, 0, j)),
            pl.BlockSpec((rows.tm, tn), lambda i, j: (i, ga0 + j)),
            pl.BlockSpec((rows.tm, tn), lambda i, j: (i, gb0 + j)),
        ],
        out_specs=pl.BlockSpec((rows.tm, tn), lambda i, j: (i, j)),
        out_shape=jax.ShapeDtypeStruct((m, d), BF16),
        compiler_params=_cparams(("parallel", "arbitrary")),
        name="merge",
    )(oa, yb, wa, wb, proj, proj)


def _resid(a, w, x, mod_arr, rows, l, v_gate, tn=512, name="resid"):
    m, kdim = a.shape
    d = x.shape[1]
    per = d // tn
    return pl.pallas_call(
        functools.partial(_resid_kernel, reps=rows.reps),
        grid=(rows.n_tiles, d // tn),
        in_specs=[
            rows.lhs_spec(kdim),
            pl.BlockSpec((None, kdim, tn), lambda i, j: (l, 0, j)),
            pl.BlockSpec((rows.tm, tn), lambda i, j: (i, j)),
            rows.mod_spec(l, tn, lambda j: v_gate * per + j),
        ],
        out_specs=pl.BlockSpec((rows.tm, tn), lambda i, j: (i, j)),
        out_shape=jax.ShapeDtypeStruct((m, d), F32),
        compiler_params=_cparams(("parallel", "arbitrary")),
        name=name,
    )(a, w, x, mod_arr)


def _ffn_in(x, mod_arr, w, l, rows, d_ff, v_scale, v_shift, tn=512):
    m, d = x.shape
    nb = d_ff // tn
    return pl.pallas_call(
        functools.partial(_norm_ffn_in_kernel, reps=rows.reps),
        grid=(rows.n_tiles, nb),
        in_specs=[
            rows.lhs_spec(d),
            rows.mod_spec(l, d, lambda j: v_scale),
            rows.mod_spec(l, d, lambda j: v_shift),
            pl.BlockSpec((None, d, tn), lambda i, j: (l, 0, j)),
            pl.BlockSpec((None, d, tn), lambda i, j: (l, 0, nb + j)),
        ],
        out_specs=pl.BlockSpec((rows.tm, tn), lambda i, j: (i, j)),
        out_shape=jax.ShapeDtypeStruct((m, d_ff), BF16),
        scratch_shapes=[pltpu.VMEM((rows.tm, d), BF16)],
        compiler_params=_cparams(("parallel", "arbitrary")),
        name="ffn_in",
    )(x, mod_arr, mod_arr, w, w)


def _scan_blocks(rows, chunk, max_chunks):
    L = min(chunk, rows.seq_len)
    assert rows.seq_len % L == 0
    n_chunks = rows.seq_len // L
    nc = min(max_chunks, n_chunks)
    assert n_chunks % nc == 0
    return L, nc, n_chunks // nc


def _gla(cfg, proj, l, rows, wg, bg, gn, state, prev, out_dtype):
    H, dk, dv = cfg.gla_heads, cfg.dk, cfg.dv
    hk, hv = H * dk, H * dv
    B = rows.n_seq
    L, nc, nt = _scan_blocks(rows, cfg.gla_chunk, 4)
    ns = 2 if (nt == 1 and nc == 1 and B % 2 == 0) else 1
    Tb = ns * L * nc
    assert cfg.off_q % hk == 0 and cfg.off_v % hv == 0
    rb = lambda b, t: b * nt + t
    in_specs = [
        pl.BlockSpec((Tb, hk), lambda b, t: (rb(b, t), cfg.off_q // hk)),
        pl.BlockSpec((Tb, hk), lambda b, t: (rb(b, t), cfg.off_k // hk)),
        pl.BlockSpec((Tb, hv), lambda b, t: (rb(b, t), cfg.off_v // hv)),
        pl.BlockSpec((Tb, LANE), lambda b, t: (rb(b, t), cfg.off_glr // LANE)),
        pl.BlockSpec((Tb, hv), lambda b, t: (rb(b, t), cfg.off_r // hv)),
        pl.BlockSpec((None, LANE, hk), lambda b, t: (l, 0, 0)),
        pl.BlockSpec((None, 1, hk), lambda b, t: (l, 0, 0)),
        pl.BlockSpec((None, 1, dv), lambda b, t: (l, 0, 0)),
    ]
    args = [proj, proj, proj, proj, proj, wg, bg, gn]
    state_spec = pl.BlockSpec((None, ns, H, dk, dv), lambda b, t: (l, b, 0, 0, 0))
    if state is not None:
        in_specs.append(state_spec)
        args.append(state)
    aliases = {}
    if prev is not None:
        in_specs.append(pl.BlockSpec(memory_space=pl.ANY))
        aliases = {len(args): 1}
        args.append(prev)
    kern = functools.partial(_gla_kernel, L=L, nc=nc, nt=nt, ns=ns, H=H, zero_init=state is None,
                             has_prev=prev is not None, scale=dk ** -0.5, inv_tau=1.0 / cfg.gla_tau)
    return pl.pallas_call(
        kern,
        grid=(B // ns, nt),
        in_specs=in_specs,
        out_specs=[pl.BlockSpec((Tb, hv), lambda b, t: (rb(b, t), 0)), state_spec],
        out_shape=[
            jax.ShapeDtypeStruct((rows.m, hv), out_dtype),
            jax.ShapeDtypeStruct((cfg.depth, B, H, dk, dv), F32),
        ],
        scratch_shapes=[pltpu.VMEM((ns * H, dk, dv), F32)],
        input_output_aliases=aliases,
        compiler_params=_cparams(("parallel", "arbitrary")),
        name="gla_scan",
    )(*args)


def _head_expander(n_heads, per):
    col_head = jnp.arange(n_heads * per, dtype=jnp.int32) // per
    ex = (col_head[None, :] == jnp.arange(LANE, dtype=jnp.int32)[:, None]).astype(BF16)
    return jnp.concatenate([ex, ex], axis=0)


def _ssd(cfg, proj, l, rows, conv_w, conv_b, dtb, alog, dsk, nrm, conv_state, state, prev, out_dtype):
    G, gw, N, P, hpg = cfg.ssd_groups, cfg.gw, cfg.ssd_state, cfg.ssd_head_dim, cfg.hpg
    inner, heads, cdim = cfg.inner, cfg.ssd_heads, cfg.conv_dim
    B = rows.n_seq
    K = cfg.ssd_conv
    L, nc, nt = _scan_blocks(rows, cfg.ssd_chunk, 2)
    Tb = L * nc
    wl = hpg * L
    gn = G * N
    assert cfg.off_z % inner == 0 and cfg.off_xbc % inner == 0 and (cfg.off_xbc + inner) % gn == 0
    rb = lambda b, t: b * nt + t
    in_specs = [
        pl.BlockSpec((Tb, inner), lambda b, t: (rb(b, t), cfg.off_z // inner)),
        pl.BlockSpec((Tb, inner), lambda b, t: (rb(b, t), cfg.off_xbc // inner)),
        pl.BlockSpec((Tb, gn), lambda b, t: (rb(b, t), (cfg.off_xbc + inner) // gn)),
        pl.BlockSpec((Tb, gn), lambda b, t: (rb(b, t), (cfg.off_xbc + inner) // gn + 1)),
        pl.BlockSpec((Tb, LANE), lambda b, t: (rb(b, t), cfg.off_dt // LANE)),
        pl.BlockSpec((None, K, cdim), lambda b, t: (l, 0, 0)),
        pl.BlockSpec((None, 1, cdim), lambda b, t: (l, 0, 0)),
        pl.BlockSpec((None, 1, LANE), lambda b, t: (l, 0, 0)),
        pl.BlockSpec((None, 1, LANE), lambda b, t: (l, 0, 0)),
        pl.BlockSpec((None, 1, inner), lambda b, t: (l, 0, 0)),
        pl.BlockSpec((None, 1, inner), lambda b, t: (l, 0, 0)),
        pl.BlockSpec((2 * LANE, inner), lambda b, t: (0, 0)),
    ]
    args = [proj, proj, proj, proj, proj, conv_w, conv_b, dtb, alog, dsk, nrm, _head_expander(heads, P)]
    has_exl = L != P
    if has_exl:
        in_specs.append(pl.BlockSpec((2 * LANE, heads * L), lambda b, t: (0, 0)))
        args.append(_head_expander(heads, L))
    state_spec = pl.BlockSpec((None, None, heads, P, N), lambda b, t: (l, b, 0, 0, 0))
    if state is not None:
        in_specs += [pl.BlockSpec((None, None, K - 1, cdim), lambda b, t: (l, b, 0, 0)), state_spec]
        args += [conv_state, state]
    aliases = {}
    if prev is not None:
        in_specs.append(pl.BlockSpec(memory_space=pl.ANY))
        aliases = {len(args): 1}
        args.append(prev)
    kern = functools.partial(_ssd_kernel, L=L, nc=nc, nt=nt, G=G, hpg=hpg, P=P, N=N, zero_init=state is None,
                             has_prev=prev is not None, has_exl=has_exl)
    xbd_dtype = BF16 if L % 16 == 0 else F32
    return pl.pallas_call(
        kern,
        grid=(B, nt),
        in_specs=in_specs,
        out_specs=[pl.BlockSpec((Tb, inner), lambda b, t: (rb(b, t), 0)), state_spec],
        out_shape=[
            jax.ShapeDtypeStruct((rows.m, inner), out_dtype),
            jax.ShapeDtypeStruct((cfg.depth, B, heads, P, N), F32),
        ],
        scratch_shapes=[
            pltpu.VMEM((Tb + SUBLANE, cdim), F32),
            pltpu.VMEM((inner if nt * nc > 1 else SUBLANE, N), F32),
            pltpu.VMEM((G * nc, wl, gw), xbd_dtype),
        ],
        input_output_aliases=aliases,
        compiler_params=_cparams(("parallel", "arbitrary")),
        name="ssd_scan",
    )(*args)


def _pack_w_in(cfg, w_in, tn):
    H, dk, dv = cfg.gla_heads, cfg.dk, cfg.dv
    sizes = (H * dk, H * dk, H * dv, H * dv, cfg.gla_rank, cfg.inner, cfg.conv_dim, cfg.ssd_heads, 2 * cfg.d_model)
    offs = [0]
    for s in sizes:
        offs.append(offs[-1] + s)
    w_t = jnp.swapaxes(w_in, 1, 2)
    seg = lambda i: w_t[:, offs[i]:offs[i + 1], :].astype(BF16)
    padto = lambda a, n: jnp.pad(a, ((0, 0), (0, n - a.shape[1]), (0, 0)))
    n_p = cfg.n_packed(tn)
    parts = [seg(5), seg(6), seg(0), seg(1), seg(2), seg(3), seg(8), padto(seg(4), LANE), padto(seg(7), LANE)]
    return padto(jnp.concatenate(parts, axis=1), n_p)


def _trunk(cfg, x, mod, rows, states, weights, scan_dtype):
    (w_in_p, wg, bg, gn, w_gla_proj, conv_w, conv_b, dtb, alog, dsk, nrm, w_ssd_proj, w_mix_out,
     w_ffn_in, w_ffn_out, final_norm, tn_in) = weights
    s_gla, s_ssm, s_conv = states
    mod_arr = rows.mod_array(mod)
    rows_e = _Rows(rows.n_seq, rows.seq_len, rows.long_seq, tm_max=256)
    B = rows.n_seq
    K = cfg.ssd_conv
    new_gla = new_ssm = None
    new_conv = []
    for l in range(cfg.depth):
        proj = _in_proj(x, mod_arr, w_in_p, l, rows, tn_in, 1, 0)
        oa, new_gla = _gla(cfg, proj, l, rows, wg, bg, gn, s_gla, new_gla, scan_dtype)
        yb, new_ssm = _ssd(cfg, proj, l, rows, conv_w, conv_b, dtb, alog, dsk, nrm, s_conv, s_ssm, new_ssm,
                           scan_dtype)
        assert rows.seq_len >= K - 1
        tail = proj.reshape(B, rows.seq_len, proj.shape[1])[:, rows.seq_len - (K - 1):, :]
        new_conv.append(tail[:, :, cfg.off_xbc:cfg.off_xbc + cfg.conv_dim])
        merged = _merge(cfg, oa, yb, proj, w_gla_proj, w_ssd_proj, l, rows)
        x = _resid(merged, w_mix_out, x, mod_arr, rows, l, 2, name="mix_out")
        act = _ffn_in(x, mod_arr, w_ffn_in, l, rows, cfg.d_ff, 4, 3)
        x = _resid(act, w_ffn_out, x, mod_arr, rows, l, 5, name="ffn_out")
    y = _final_norm(x, final_norm, rows_e)
    return y, new_gla, new_ssm, jnp.stack(new_conv)


def _forward(cfg, x_prompt, x_sample, c_prompt, c_sample, state_gla, state_ssm, state_conv,
             w_ada, b_ada, w_in, w_gla_gate, b_gla_gate, gla_norm, w_gla_proj, conv_w, conv_b,
             dt_bias, A_log, d_skip, ssd_norm, w_ssd_proj, w_mix_out, w_ffn_in, w_ffn_out, final_norm):
    depth, d = cfg.depth, cfg.d_model
    bp, tp, _ = x_prompt.shape
    bs, ts, _ = x_sample.shape
    H, dk, dv = cfg.gla_heads, cfg.dk, cfg.dv
    P = cfg.ssd_head_dim
    assert cfg.ssd_heads <= LANE and cfg.gla_rank <= LANE
    assert cfg.off_gates % 256 == 0

    pad = (-bp) % SUBLANE
    c_all = jnp.concatenate([c_prompt, jnp.zeros((pad, d), F32), c_sample], axis=0)
    mod = _ada(c_all, w_ada, b_ada)
    mod_p = mod[:, :bp]
    mod_s = mod[:, bp + pad:]

    tn_in = 768
    padl = lambda a: jnp.pad(a, ((0, 0), (0, LANE - a.shape[1])))
    weights = (
        _pack_w_in(cfg, w_in, tn_in),
        jnp.pad(w_gla_gate, ((0, 0), (0, LANE - cfg.gla_rank), (0, 0))).astype(BF16),
        b_gla_gate.reshape(depth, 1, H * dk),
        gla_norm.reshape(depth, 1, dv),
        w_gla_proj.astype(BF16),
        conv_w,
        conv_b.reshape(depth, 1, cfg.conv_dim),
        padl(dt_bias).reshape(depth, 1, LANE),
        padl(A_log).reshape(depth, 1, LANE),
        jnp.repeat(d_skip, P, axis=1).reshape(depth, 1, cfg.inner),
        ssd_norm.reshape(depth, 1, cfg.inner),
        w_ssd_proj.astype(BF16),
        w_mix_out.astype(BF16),
        w_ffn_in.astype(BF16),
        w_ffn_out.astype(BF16),
        final_norm,
        tn_in,
    )
    rows_p = _Rows(bp, tp, True)
    rows_s = _Rows(bs, ts, False)
    y_p, gla_p, ssm_p, conv_p = _trunk(cfg, x_prompt.reshape(bp * tp, d), mod_p, rows_p,
                                       (None, None, None), weights, BF16)
    states_s = (state_gla, state_ssm, state_conv)
    y_s, gla_s, ssm_s, conv_s = _trunk(cfg, x_sample.reshape(bs * ts, d), mod_s, rows_s,
                                       states_s, weights, F32)
    return (y_p.reshape(bp, tp, d), y_s.reshape(bs, ts, d), gla_p, ssm_p, conv_p, gla_s, ssm_s, conv_s)


def kernel(x_prompt, x_sample, c_prompt, c_sample, state_gla, state_ssm, state_conv, w_ada, b_ada, w_in,
           w_gla_gate, b_gla_gate, gla_norm, w_gla_proj, conv_w, conv_b, dt_bias, A_log, d_skip, ssd_norm,
           w_ssd_proj, w_mix_out, w_ffn_in, w_ffn_out, final_norm):
    cfg = Cfg()
    assert x_prompt.shape[2] == cfg.d_model
    return _forward(cfg, x_prompt, x_sample, c_prompt, c_sample, state_gla, state_ssm, state_conv,
                    w_ada, b_ada, w_in, w_gla_gate, b_gla_gate, gla_norm, w_gla_proj, conv_w, conv_b,
                    dt_bias, A_log, d_skip, ssd_norm, w_ssd_proj, w_mix_out, w_ffn_in, w_ffn_out, final_norm)
```

```python
import dataclasses
import functools

import jax
import jax.numpy as jnp
from jax import lax
from jax.experimental import pallas as pl
from jax.experimental.pallas import tpu as pltpu

F32 = jnp.float32
BF16 = jnp.bfloat16

LANE = 128
SUBLANE = 8
VMEM_LIMIT = 56 * 1024 * 1024
EPS = 1e-6


@dataclasses.dataclass(frozen=True)
class Cfg:
    d_model: int = 2048
    depth: int = 2
    gla_heads: int = 4
    gla_rank: int = 16
    gla_tau: float = 16.0
    gla_chunk: int = 64
    ssd_head_dim: int = 64
    ssd_groups: int = 8
    ssd_state: int = 128
    ssd_conv: int = 4
    ssd_chunk: int = 64

    @property
    def dk(self):
        return self.d_model // 2 // self.gla_heads

    @property
    def dv(self):
        return self.d_model // self.gla_heads

    @property
    def inner(self):
        return 2 * self.d_model

    @property
    def ssd_heads(self):
        return self.inner // self.ssd_head_dim

    @property
    def hpg(self):
        return self.ssd_heads // self.ssd_groups

    @property
    def gw(self):
        return self.inner // self.ssd_groups

    @property
    def conv_dim(self):
        return self.inner + 2 * self.ssd_groups * self.ssd_state

    @property
    def d_ff(self):
        return ((8 * self.d_model // 3 + 255) // 256) * 256

    @property
    def off_z(self):
        return 0

    @property
    def off_xbc(self):
        return self.inner

    @property
    def off_q(self):
        return self.off_xbc + self.conv_dim

    @property
    def off_k(self):
        return self.off_q + self.gla_heads * self.dk

    @property
    def off_v(self):
        return self.off_k + self.gla_heads * self.dk

    @property
    def off_r(self):
        return self.off_v + self.gla_heads * self.dv

    @property
    def off_gates(self):
        return self.off_r + self.gla_heads * self.dv

    @property
    def off_glr(self):
        return self.off_gates + 2 * self.d_model

    @property
    def off_dt(self):
        return self.off_glr + LANE

    def n_packed(self, tn):
        n = self.off_dt + LANE
        return -(-n // tn) * tn


def _cparams(sem):
    return pltpu.CompilerParams(dimension_semantics=sem, vmem_limit_bytes=VMEM_LIMIT)


def _sigmoid(x):
    return 1.0 / (1.0 + jnp.exp(-x))


def _silu(x):
    hx = 0.5 * x
    return hx + hx * jnp.tanh(hx)


def _softplus(x):
    return jnp.maximum(x, 0.0) + jnp.log1p(jnp.exp(-jnp.abs(x)))


def _log_sigmoid(x):
    return jnp.minimum(x, 0.0) - jnp.log(1.0 + jnp.exp(-jnp.abs(x)))


def _dot(a, b):
    return jnp.dot(a, b, preferred_element_type=F32)


def _dot_nt(a, b):
    return lax.dot_general(a, b, (((1,), (1,)), ((), ())), preferred_element_type=F32)


def _dot_tn(a, b):
    return lax.dot_general(a, b, (((0,), (0,)), ((), ())), preferred_element_type=F32)


def _split(x, parts):
    out = []
    r = x
    for i in range(parts):
        p = r.astype(BF16)
        out.append(p)
        if i + 1 < parts:
            r = r - p.astype(F32)
    return out


def _split_rows(x):
    hi = x.astype(BF16).astype(F32)
    return jnp.concatenate([hi, x - hi], axis=0).astype(BF16)


def _split_lanes(x):
    hi = x.astype(BF16)
    return jnp.concatenate([hi, (x - hi.astype(F32)).astype(BF16)], axis=1)


def _rms(x):
    return x * lax.rsqrt(jnp.mean(x * x, axis=-1, keepdims=True) + EPS)


def _expand_rows(v, reps):
    if reps == 1:
        return v
    g = v.shape[0]
    r = lax.broadcasted_iota(jnp.int32, (g * reps, g), 0)
    c = lax.broadcasted_iota(jnp.int32, (g * reps, g), 1)
    lo = c * reps
    e = jnp.where((r >= lo) & (r < lo + reps), 1.0, 0.0).astype(BF16)
    acc = None
    for p in _split(v, 3):
        d = _dot(e, p)
        acc = d if acc is None else acc + d
    return acc


def _tri(n):
    r = lax.broadcasted_iota(jnp.int32, (n, n), 0)
    c = lax.broadcasted_iota(jnp.int32, (n, n), 1)
    return r >= c


def _ada_kernel(c_ref, w_ref, b_ref, o_ref):
    a = _silu(c_ref[...]).astype(BF16)
    o_ref[...] = _dot(a, w_ref[...].astype(BF16)) + b_ref[...]


NORM_ROWS = 256


def _modulated_norm(x_ref, sc_ref, sh_ref, h_scr, reps):
    @pl.when(pl.program_id(1) == 0)
    def _():
        tm = x_ref.shape[0]
        step = min(NORM_ROWS, tm)
        for r in range(0, tm, step):
            rs = slice(r, r + step)
            if reps == 1:
                sc, sh = sc_ref[...], sh_ref[...]
            else:
                ms = slice(r // reps, (r + step) // reps)
                sc = _expand_rows(sc_ref[ms, :], reps)
                sh = _expand_rows(sh_ref[ms, :], reps)
            h_scr[rs, :] = (_rms(x_ref[rs, :]) * (1.0 + sc) + sh).astype(h_scr.dtype)


def _norm_proj_kernel(x_ref, sc_ref, sh_ref, w_ref, o_ref, h_scr, *, reps):
    _modulated_norm(x_ref, sc_ref, sh_ref, h_scr, reps)
    o_ref[...] = _dot_nt(h_scr[...], w_ref[...])


def _norm_ffn_in_kernel(x_ref, sc_ref, sh_ref, wg_ref, wu_ref, o_ref, h_scr, *, reps):
    _modulated_norm(x_ref, sc_ref, sh_ref, h_scr, reps)
    h = h_scr[...]
    o_ref[...] = (_silu(_dot(h, wg_ref[...])) * _dot(h, wu_ref[...])).astype(o_ref.dtype)


def _final_norm_kernel(x_ref, g_ref, o_ref):
    o_ref[...] = _rms(x_ref[...]) * g_ref[...]


def _merge_kernel(oa_ref, yb_ref, wa_ref, wb_ref, ga_ref, gb_ref, o_ref):
    ya = _dot(oa_ref[...].astype(BF16), wa_ref[...])
    yb = _dot(yb_ref[...].astype(BF16), wb_ref[...])
    o_ref[...] = (_sigmoid(ga_ref[...]) * ya + _sigmoid(gb_ref[...]) * yb).astype(o_ref.dtype)


def _resid_kernel(a_ref, w_ref, x_ref, g_ref, o_ref, *, reps):
    y = _dot(a_ref[...], w_ref[...])
    o_ref[...] = x_ref[...] + _expand_rows(g_ref[...], reps) * y


def _gla_kernel(*refs, L, nc, nt, ns, H, zero_init, has_prev, scale, inv_tau):
    q_ref, k_ref, v_ref, glr_ref, r_ref, wg_ref, bg_ref, gn_ref = refs[:8]
    i = 8
    s0_ref = None
    if not zero_init:
        s0_ref = refs[i]
        i += 1
    if has_prev:
        i += 1
    o_ref, s_out_ref, s_scr = refs[i:i + 3]
    t = pl.program_id(1)
    _, dk, dv = s_scr.shape

    @pl.when(t == 0)
    def _init():
        if zero_init:
            s_scr[...] = jnp.zeros_like(s_scr)
        else:
            s_scr[...] = s0_ref[...].reshape(s_scr.shape)

    ga = _dot(glr_ref[...].astype(BF16), wg_ref[...]) + bg_ref[...]
    log_a = _log_sigmoid(ga) * inv_tau
    tri = jnp.where(_tri(L), 1.0, 0.0).astype(BF16)
    tri2 = jnp.concatenate([tri, tri], axis=1)
    mask = _tri(L)
    ones2 = jnp.ones((2 * L, LANE), BF16)
    heads = range(H)
    ck = [slice(h * dk, (h + 1) * dk) for h in heads]
    cv = [slice(h * dv, (h + 1) * dv) for h in heads]
    for unit in range(ns * nc):
        sl = slice(unit * L, (unit + 1) * L)
        s0 = (unit // nc) * H
        g2 = _split_rows(log_a[sl, :])
        b = _dot(tri2, g2)
        tot = [_dot_tn(g2[:, ck[h]], ones2) for h in heads]
        eb = jnp.exp(b)
        q_in = (q_ref[sl, :] * scale * eb).astype(BF16)
        k = k_ref[sl, :]
        k_in = (k * jnp.exp(-b)).astype(BF16)
        k_out = (k * jnp.exp(b[L - 1:L, :] - b)).astype(BF16)
        v = [v_ref[sl, cv[h]].astype(BF16) for h in heads]
        att = [jnp.where(mask, _dot_nt(q_in[:, ck[h]], k_in[:, ck[h]]), 0.0).astype(BF16) for h in heads]
        s = [s_scr[s0 + h] for h in heads]
        o = [_dot(q_in[:, ck[h]], s[h].astype(BF16)) + _dot(att[h], v[h]) for h in heads]
        upd = [_dot_tn(k_out[:, ck[h]], v[h]) for h in heads]
        for h in heads:
            decay = jnp.exp(tot[h])
            s_scr[s0 + h] = s[h] * jnp.concatenate([decay] * (dv // LANE), axis=1) + upd[h]
            on = _rms(o[h]) * gn_ref[...]
            o_ref[sl, cv[h]] = (on * _silu(r_ref[sl, cv[h]])).astype(o_ref.dtype)

    @pl.when(t == nt - 1)
    def _fin():
        s_out_ref[...] = s_scr[...].reshape(s_out_ref.shape)


def _ssd_kernel(*refs, L, nc, nt, G, hpg, P, N, zero_init, has_prev, has_exl):
    (z_ref, x_ref, b_ref, c_ref, dt_ref, cw_ref, cb_ref, dtb_ref, alog_ref, dsk_ref, nrm_ref, exp_ref) = refs[:12]
    i = 12
    exl_ref = exp_ref
    if has_exl:
        exl_ref = refs[i]
        i += 1
    cs_ref = h0_ref = None
    if not zero_init:
        cs_ref, h0_ref = refs[i:i + 2]
        i += 2
    if has_prev:
        i += 1
    y_ref, h_out_ref, u, h_scr, xbd = refs[i:i + 5]
    t = pl.program_id(1)
    Tb = L * nc
    gw = hpg * P
    inner = G * gw
    wl = hpg * L
    nconv = cw_ref.shape[0]
    halo = SUBLANE

    direct = nt == 1 and nc == 1

    @pl.when(t == 0)
    def _init():
        xbd[...] = jnp.zeros_like(xbd)
        u[0:halo, :] = jnp.zeros((halo, u.shape[1]), F32)
        if not zero_init:
            u[halo - (nconv - 1):halo, :] = cs_ref[...]
        if not direct:
            if zero_init:
                h_scr[...] = jnp.zeros_like(h_scr)
            else:
                h_scr[...] = h0_ref[...].reshape(inner, N)

    def load_h(g):
        if not direct:
            return h_scr[g * gw:(g + 1) * gw, :]
        if zero_init:
            return jnp.zeros((gw, N), F32)
        return h0_ref[g * hpg:(g + 1) * hpg].reshape(gw, N)

    def store_h(g, val):
        if direct:
            h_out_ref[g * hpg:(g + 1) * hpg] = val.reshape(hpg, P, N)
        else:
            h_scr[g * gw:(g + 1) * gw, :] = val

    u[halo:halo + Tb, 0:inner] = x_ref[...]
    u[halo:halo + Tb, inner:inner + G * N] = b_ref[...]
    u[halo:halo + Tb, inner + G * N:inner + 2 * G * N] = c_ref[...]

    def conv(r0, lo, width):
        win = u[pl.ds(r0, L + halo), lo:lo + width]
        acc = cb_ref[:, lo:lo + width] + win[halo:, :] * cw_ref[nconv - 1:nconv, lo:lo + width]
        for j in range(nconv - 1):
            moved = pltpu.roll(win, nconv - 1 - j, 0)[halo:, :]
            acc = acc + moved * cw_ref[j:j + 1, lo:lo + width]
        return _silu(acc)

    dtp = _softplus(dt_ref[...] + dtb_ref[...])
    a_h = dtp * (-jnp.exp(alog_ref[...]))

    tri = jnp.where(_tri(L), 1.0, 0.0).astype(BF16)
    tri2 = jnp.concatenate([tri, tri], axis=1)
    ones2 = jnp.ones((2 * L, LANE), BF16)
    row_l = lax.broadcasted_iota(jnp.int32, (L, G * wl), 0)
    s_l = lax.broadcasted_iota(jnp.int32, (L, G * wl), 1) % L
    later = row_l > s_l
    causal = row_l >= s_l
    early_cast = L % 16 == 0
    lane_in_piece = lax.broadcasted_iota(jnp.int32, (L, LANE), 1)
    gc = [slice(g * gw, (g + 1) * gw) for g in range(G)]
    gbatch = 2 if L >= P else G
    assert G % gbatch == 0

    for c in range(nc):
        sl = slice(c * L, (c + 1) * L)
        da2 = _split_lanes(jnp.concatenate([dtp[sl], a_h[sl]], axis=0))
        da_exp = _dot(da2, exp_ref[...])
        dt_exp = da_exp[:L]
        a_exp_p = da_exp[L:]
        a_exp_l = _dot(_split_lanes(a_h[sl]), exl_ref[...]) if has_exl else a_exp_p
        ap2 = _split_rows(a_exp_p)
        al2 = _split_rows(jnp.where(later, a_exp_l, 0.0))
        ccol = _dot(tri2, ap2)
        seg = _dot(tri2, al2)
        decay = jnp.where(causal, jnp.exp(seg), 0.0)
        grow = jnp.exp(ccol)
        fade = jnp.exp(ccol[L - 1:L, :] - ccol)
        lp = gc
        ll = [slice(g * wl, (g + 1) * wl) for g in range(G)]

        for g0 in range(0, G, gbatch):
            batch = range(g0, g0 + gbatch)
            xs, bs, w_cats, inters = {}, {}, {}, {}
            for g in batch:
                x = conv(c * L, g * gw, gw)
                b = conv(c * L, inner + g * N, N)
                cm = conv(c * L, inner + G * N + g * N, N).astype(BF16)
                if early_cast:
                    b_t = jnp.concatenate([b.astype(BF16)] * hpg, axis=0)
                else:
                    b_t = jnp.concatenate([b] * hpg, axis=0).astype(BF16)
                w_cats[g] = (_dot_nt(cm, b_t) * decay[:, ll[g]]).astype(BF16)
                inters[g] = _dot_nt(cm, load_h(g).astype(BF16))
                xs[g] = x
                bs[g] = b

            xts, intras = {}, {}
            for g in batch:
                xt = xs[g] * dt_exp[:, lp[g]]
                xt_c = xt.astype(xbd.dtype)
                slab = g * nc + c
                for hh in range(hpg):
                    lo = (hh * P) // LANE * LANE
                    piece = xt_c[:, lo:lo + LANE]
                    own = (lane_in_piece >= hh * P - lo) & (lane_in_piece < (hh + 1) * P - lo)
                    xbd[slab, hh * L:(hh + 1) * L, lo:lo + LANE] = jnp.where(own, piece, jnp.zeros_like(piece))
                intras[g] = _dot(w_cats[g], xbd[slab].astype(BF16))
                xts[g] = xt

            for g in batch:
                x_dl = (xts[g] * fade[:, lp[g]]).astype(BF16)
                tot = _dot_tn(ap2[:, lp[g]], ones2)
                h = load_h(g)
                store_h(g, h * jnp.concatenate([jnp.exp(tot)] * (N // LANE), axis=1)
                        + _dot_tn(x_dl, bs[g].astype(BF16)))

            for g in batch:
                y = intras[g] + grow[:, lp[g]] * inters[g] + dsk_ref[:, gc[g]] * xs[g]
                yg = y * _silu(z_ref[sl, gc[g]])
                y_ref[sl, gc[g]] = (_rms(yg) * nrm_ref[:, gc[g]]).astype(y_ref.dtype)

    u[0:halo, :] = u[Tb:Tb + halo, :]

    if not direct:
        @pl.when(t == nt - 1)
        def _fin():
            h_out_ref[...] = h_scr[...].reshape(h_out_ref.shape)


def _ada(c_all, w_ada, b_ada, tn=1024):
    depth, d, n = w_ada.shape
    m = c_all.shape[0]
    return pl.pallas_call(
        _ada_kernel,
        grid=(depth, n // tn),
        in_specs=[
            pl.BlockSpec((m, d), lambda l, j: (0, 0)),
            pl.BlockSpec((None, d, tn), lambda l, j: (l, 0, j)),
            pl.BlockSpec((None, 1, tn), lambda l, j: (l, 0, j)),
        ],
        out_specs=pl.BlockSpec((None, m, tn), lambda l, j: (l, 0, j)),
        out_shape=jax.ShapeDtypeStruct((depth, m, n), F32),
        compiler_params=_cparams(("parallel", "parallel")),
        name="ada",
    )(c_all, w_ada, b_ada.reshape(depth, 1, n))


class _Rows:
    def __init__(self, n_seq, seq_len, long_seq, tm_max=1024):
        self.n_seq, self.seq_len, self.long_seq = n_seq, seq_len, long_seq
        self.m = n_seq * seq_len
        if long_seq:
            self.tm = min(tm_max, seq_len)
            assert seq_len % self.tm == 0
            self.reps = 1
            self.mod_rows = 1
        else:
            self.tm = min(self.m, tm_max)
            assert self.m % self.tm == 0 and self.tm % seq_len == 0
            self.reps = seq_len
            self.mod_rows = self.tm // seq_len
        self.n_tiles = self.m // self.tm

    def lhs_spec(self, width):
        if self.n_tiles == 1:
            return pl.BlockSpec((self.tm, width), lambda i, j: (i, 0), pipeline_mode=pl.Buffered(1))
        return pl.BlockSpec((self.tm, width), lambda i, j: (i, 0))

    def mod_array(self, mod):
        depth, _, n = mod.shape
        if self.reps == 1:
            return mod.reshape(depth, self.n_seq, 1, n)
        return mod

    def mod_spec(self, l, tn, col_block):
        if self.reps == 1:
            per = self.seq_len // self.tm
            return pl.BlockSpec((None, None, 1, tn), lambda i, j: (l, i // per, 0, col_block(j)))
        return pl.BlockSpec((None, self.mod_rows, tn), lambda i, j: (l, i, col_block(j)))


def _final_norm(x, gain, rows):
    m, d = x.shape
    return pl.pallas_call(
        _final_norm_kernel,
        grid=(rows.n_tiles,),
        in_specs=[pl.BlockSpec((rows.tm, d), lambda i: (i, 0)), pl.BlockSpec((1, d), lambda i: (0, 0))],
        out_specs=pl.BlockSpec((rows.tm, d), lambda i: (i, 0)),
        out_shape=jax.ShapeDtypeStruct((m, d), F32),
        compiler_params=_cparams(("parallel",)),
        name="final_norm",
    )(x, gain.reshape(1, d))


def _in_proj(x, mod_arr, w_t, l, rows, tn, v_scale, v_shift):
    m, d = x.shape
    n = w_t.shape[1]
    return pl.pallas_call(
        functools.partial(_norm_proj_kernel, reps=rows.reps),
        grid=(rows.n_tiles, n // tn),
        in_specs=[
            rows.lhs_spec(d),
            rows.mod_spec(l, d, lambda j: v_scale),
            rows.mod_spec(l, d, lambda j: v_shift),
            pl.BlockSpec((None, tn, d), lambda i, j: (l, j, 0)),
        ],
        out_specs=pl.BlockSpec((rows.tm, tn), lambda i, j: (i, j)),
        out_shape=jax.ShapeDtypeStruct((m, n), F32),
        scratch_shapes=[pltpu.VMEM((rows.tm, d), BF16)],
        compiler_params=_cparams(("parallel", "arbitrary")),
        name="in_proj",
    )(x, mod_arr, mod_arr, w_t)


def _merge(cfg, oa, yb, proj, wa, wb, l, rows):
    m = oa.shape[0]
    d = cfg.d_model
    tn = 512 if oa.dtype == BF16 else 256
    ga0 = cfg.off_gates // tn
    gb0 = (cfg.off_gates + d) // tn
    return pl.pallas_call(
        _merge_kernel,
        grid=(rows.n_tiles, d // tn),
        in_specs=[
            rows.lhs_spec(oa.shape[1]),
            rows.lhs_spec(yb.shape[1]),
            pl.BlockSpec((None, wa.shape[1], tn), lambda i, j: (l, 0, j)),
            pl.BlockSpec((None, wb.shape[1], tn), lambda i, j: (l, 0, j)),
            pl.BlockSpec((rows.tm, tn), lambda i, j: (i, ga0 + j)),
            pl.BlockSpec((rows.tm, tn), lambda i, j: (i, gb0 + j)),
        ],
        out_specs=pl.BlockSpec((rows.tm, tn), lambda i, j: (i, j)),
        out_shape=jax.ShapeDtypeStruct((m, d), BF16),
        compiler_params=_cparams(("parallel", "arbitrary")),
        name="merge",
    )(oa, yb, wa, wb, proj, proj)


def _resid(a, w, x, mod_arr, rows, l, v_gate, tn=512, name="resid"):
    m, kdim = a.shape
    d = x.shape[1]
    per = d // tn
    return pl.pallas_call(
        functools.partial(_resid_kernel, reps=rows.reps),
        grid=(rows.n_tiles, d // tn),
        in_specs=[
            rows.lhs_spec(kdim),
            pl.BlockSpec((None, kdim, tn), lambda i, j: (l, 0, j)),
            pl.BlockSpec((rows.tm, tn), lambda i, j: (i, j)),
            rows.mod_spec(l, tn, lambda j: v_gate * per + j),
        ],
        out_specs=pl.BlockSpec((rows.tm, tn), lambda i, j: (i, j)),
        out_shape=jax.ShapeDtypeStruct((m, d), F32),
        compiler_params=_cparams(("parallel", "arbitrary")),
        name=name,
    )(a, w, x, mod_arr)


def _ffn_in(x, mod_arr, w, l, rows, d_ff, v_scale, v_shift, tn=512):
    m, d = x.shape
    nb = d_ff // tn
    return pl.pallas_call(
        functools.partial(_norm_ffn_in_kernel, reps=rows.reps),
        grid=(rows.n_tiles, nb),
        in_specs=[
            rows.lhs_spec(d),
            rows.mod_spec(l, d, lambda j: v_scale),
            rows.mod_spec(l, d, lambda j: v_shift),
            pl.BlockSpec((None, d, tn), lambda i, j: (l, 0, j)),
            pl.BlockSpec((None, d, tn), lambda i, j: (l, 0, nb + j)),
        ],
        out_specs=pl.BlockSpec((rows.tm, tn), lambda i, j: (i, j)),
        out_shape=jax.ShapeDtypeStruct((m, d_ff), BF16),
        scratch_shapes=[pltpu.VMEM((rows.tm, d), BF16)],
        compiler_params=_cparams(("parallel", "arbitrary")),
        name="ffn_in",
    )(x, mod_arr, mod_arr, w, w)


def _scan_blocks(rows, chunk, max_chunks):
    L = min(chunk, rows.seq_len)
    assert rows.seq_len % L == 0
    n_chunks = rows.seq_len // L
    nc = min(max_chunks, n_chunks)
    assert n_chunks % nc == 0
    return L, nc, n_chunks // nc


def _gla(cfg, proj, l, rows, wg, bg, gn, state, prev, out_dtype):
    H, dk, dv = cfg.gla_heads, cfg.dk, cfg.dv
    hk, hv = H * dk, H * dv
    B = rows.n_seq
    L, nc, nt = _scan_blocks(rows, cfg.gla_chunk, 4)
    ns = 1
    if nt == 1 and nc == 1:
        ns = 4 if B % 4 == 0 else (2 if B % 2 == 0 else 1)
    Tb = ns * L * nc
    assert cfg.off_q % hk == 0 and cfg.off_v % hv == 0
    rb = lambda b, t: b * nt + t
    in_specs = [
        pl.BlockSpec((Tb, hk), lambda b, t: (rb(b, t), cfg.off_q // hk)),
        pl.BlockSpec((Tb, hk), lambda b, t: (rb(b, t), cfg.off_k // hk)),
        pl.BlockSpec((Tb, hv), lambda b, t: (rb(b, t), cfg.off_v // hv)),
        pl.BlockSpec((Tb, LANE), lambda b, t: (rb(b, t), cfg.off_glr // LANE)),
        pl.BlockSpec((Tb, hv), lambda b, t: (rb(b, t), cfg.off_r // hv)),
        pl.BlockSpec((None, LANE, hk), lambda b, t: (l, 0, 0)),
        pl.BlockSpec((None, 1, hk), lambda b, t: (l, 0, 0)),
        pl.BlockSpec((None, 1, dv), lambda b, t: (l, 0, 0)),
    ]
    args = [proj, proj, proj, proj, proj, wg, bg, gn]
    state_spec = pl.BlockSpec((None, ns, H, dk, dv), lambda b, t: (l, b, 0, 0, 0))
    if state is not None:
        in_specs.append(state_spec)
        args.append(state)
    aliases = {}
    if prev is not None:
        in_specs.append(pl.BlockSpec(memory_space=pl.ANY))
        aliases = {len(args): 1}
        args.append(prev)
    kern = functools.partial(_gla_kernel, L=L, nc=nc, nt=nt, ns=ns, H=H, zero_init=state is None,
                             has_prev=prev is not None, scale=dk ** -0.5, inv_tau=1.0 / cfg.gla_tau)
    return pl.pallas_call(
        kern,
        grid=(B // ns, nt),
        in_specs=in_specs,
        out_specs=[pl.BlockSpec((Tb, hv), lambda b, t: (rb(b, t), 0)), state_spec],
        out_shape=[
            jax.ShapeDtypeStruct((rows.m, hv), out_dtype),
            jax.ShapeDtypeStruct((cfg.depth, B, H, dk, dv), F32),
        ],
        scratch_shapes=[pltpu.VMEM((ns * H, dk, dv), F32)],
        input_output_aliases=aliases,
        compiler_params=_cparams(("parallel", "arbitrary")),
        name="gla_scan",
    )(*args)


def _head_expander(n_heads, per):
    col_head = jnp.arange(n_heads * per, dtype=jnp.int32) // per
    ex = (col_head[None, :] == jnp.arange(LANE, dtype=jnp.int32)[:, None]).astype(BF16)
    return jnp.concatenate([ex, ex], axis=0)


def _ssd(cfg, proj, l, rows, conv_w, conv_b, dtb, alog, dsk, nrm, conv_state, state, prev, out_dtype):
    G, gw, N, P, hpg = cfg.ssd_groups, cfg.gw, cfg.ssd_state, cfg.ssd_head_dim, cfg.hpg
    inner, heads, cdim = cfg.inner, cfg.ssd_heads, cfg.conv_dim
    B = rows.n_seq
    K = cfg.ssd_conv
    L, nc, nt = _scan_blocks(rows, cfg.ssd_chunk, 2)
    Tb = L * nc
    wl = hpg * L
    gn = G * N
    assert cfg.off_z % inner == 0 and cfg.off_xbc % inner == 0 and (cfg.off_xbc + inner) % gn == 0
    rb = lambda b, t: b * nt + t
    in_specs = [
        pl.BlockSpec((Tb, inner), lambda b, t: (rb(b, t), cfg.off_z // inner)),
        pl.BlockSpec((Tb, inner), lambda b, t: (rb(b, t), cfg.off_xbc // inner)),
        pl.BlockSpec((Tb, gn), lambda b, t: (rb(b, t), (cfg.off_xbc + inner) // gn)),
        pl.BlockSpec((Tb, gn), lambda b, t: (rb(b, t), (cfg.off_xbc + inner) // gn + 1)),
        pl.BlockSpec((Tb, LANE), lambda b, t: (rb(b, t), cfg.off_dt // LANE)),
        pl.BlockSpec((None, K, cdim), lambda b, t: (l, 0, 0)),
        pl.BlockSpec((None, 1, cdim), lambda b, t: (l, 0, 0)),
        pl.BlockSpec((None, 1, LANE), lambda b, t: (l, 0, 0)),
        pl.BlockSpec((None, 1, LANE), lambda b, t: (l, 0, 0)),
        pl.BlockSpec((None, 1, inner), lambda b, t: (l, 0, 0)),
        pl.BlockSpec((None, 1, inner), lambda b, t: (l, 0, 0)),
        pl.BlockSpec((2 * LANE, inner), lambda b, t: (0, 0)),
    ]
    args = [proj, proj, proj, proj, proj, conv_w, conv_b, dtb, alog, dsk, nrm, _head_expander(heads, P)]
    has_exl = L != P
    if has_exl:
        in_specs.append(pl.BlockSpec((2 * LANE, heads * L), lambda b, t: (0, 0)))
        args.append(_head_expander(heads, L))
    state_spec = pl.BlockSpec((None, None, heads, P, N), lambda b, t: (l, b, 0, 0, 0))
    if state is not None:
        in_specs += [pl.BlockSpec((None, None, K - 1, cdim), lambda b, t: (l, b, 0, 0)), state_spec]
        args += [conv_state, state]
    aliases = {}
    if prev is not None:
        in_specs.append(pl.BlockSpec(memory_space=pl.ANY))
        aliases = {len(args): 1}
        args.append(prev)
    kern = functools.partial(_ssd_kernel, L=L, nc=nc, nt=nt, G=G, hpg=hpg, P=P, N=N, zero_init=state is None,
                             has_prev=prev is not None, has_exl=has_exl)
    xbd_dtype = BF16 if L % 16 == 0 else F32
    return pl.pallas_call(
        kern,
        grid=(B, nt),
        in_specs=in_specs,
        out_specs=[pl.BlockSpec((Tb, inner), lambda b, t: (rb(b, t), 0)), state_spec],
        out_shape=[
            jax.ShapeDtypeStruct((rows.m, inner), out_dtype),
            jax.ShapeDtypeStruct((cfg.depth, B, heads, P, N), F32),
        ],
        scratch_shapes=[
            pltpu.VMEM((Tb + SUBLANE, cdim), F32),
            pltpu.VMEM((inner if nt * nc > 1 else SUBLANE, N), F32),
            pltpu.VMEM((G * nc, wl, gw), xbd_dtype),
        ],
        input_output_aliases=aliases,
        compiler_params=_cparams(("parallel", "arbitrary")),
        name="ssd_scan",
    )(*args)


def _pack_w_in(cfg, w_in, tn):
    H, dk, dv = cfg.gla_heads, cfg.dk, cfg.dv
    sizes = (H * dk, H * dk, H * dv, H * dv, cfg.gla_rank, cfg.inner, cfg.conv_dim, cfg.ssd_heads, 2 * cfg.d_model)
    offs = [0]
    for s in sizes:
        offs.append(offs[-1] + s)
    w_t = jnp.swapaxes(w_in, 1, 2)
    seg = lambda i: w_t[:, offs[i]:offs[i + 1], :].astype(BF16)
    padto = lambda a, n: jnp.pad(a, ((0, 0), (0, n - a.shape[1]), (0, 0)))
    n_p = cfg.n_packed(tn)
    parts = [seg(5), seg(6), seg(0), seg(1), seg(2), seg(3), seg(8), padto(seg(4), LANE), padto(seg(7), LANE)]
    return padto(jnp.concatenate(parts, axis=1), n_p)


def _trunk(cfg, x, mod, rows, states, weights, scan_dtype):
    (w_in_p, wg, bg, gn, w_gla_proj, conv_w, conv_b, dtb, alog, dsk, nrm, w_ssd_proj, w_mix_out,
     w_ffn_in, w_ffn_out, final_norm, tn_in) = weights
    s_gla, s_ssm, s_conv = states
    mod_arr = rows.mod_array(mod)
    rows_e = _Rows(rows.n_seq, rows.seq_len, rows.long_seq, tm_max=512)
    B = rows.n_seq
    K = cfg.ssd_conv
    new_gla = new_ssm = None
    new_conv = []
    for l in range(cfg.depth):
        proj = _in_proj(x, mod_arr, w_in_p, l, rows, tn_in, 1, 0)
        oa, new_gla = _gla(cfg, proj, l, rows, wg, bg, gn, s_gla, new_gla, scan_dtype)
        yb, new_ssm = _ssd(cfg, proj, l, rows, conv_w, conv_b, dtb, alog, dsk, nrm, s_conv, s_ssm, new_ssm,
                           scan_dtype)
        assert rows.seq_len >= K - 1
        tail = proj.reshape(B, rows.seq_len, proj.shape[1])[:, rows.seq_len - (K - 1):, :]
        new_conv.append(tail[:, :, cfg.off_xbc:cfg.off_xbc + cfg.conv_dim])
        merged = _merge(cfg, oa, yb, proj, w_gla_proj, w_ssd_proj, l, rows)
        x = _resid(merged, w_mix_out, x, mod_arr, rows, l, 2, name="mix_out")
        act = _ffn_in(x, mod_arr, w_ffn_in, l, rows, cfg.d_ff, 4, 3)
        x = _resid(act, w_ffn_out, x, mod_arr, rows, l, 5, name="ffn_out")
    y = _final_norm(x, final_norm, rows_e)
    return y, new_gla, new_ssm, jnp.stack(new_conv)


def _forward(cfg, x_prompt, x_sample, c_prompt, c_sample, state_gla, state_ssm, state_conv,
             w_ada, b_ada, w_in, w_gla_gate, b_gla_gate, gla_norm, w_gla_proj, conv_w, conv_b,
             dt_bias, A_log, d_skip, ssd_norm, w_ssd_proj, w_mix_out, w_ffn_in, w_ffn_out, final_norm):
    depth, d = cfg.depth, cfg.d_model
    bp, tp, _ = x_prompt.shape
    bs, ts, _ = x_sample.shape
    H, dk, dv = cfg.gla_heads, cfg.dk, cfg.dv
    P = cfg.ssd_head_dim
    assert cfg.ssd_heads <= LANE and cfg.gla_rank <= LANE
    assert cfg.off_gates % 256 == 0

    pad = (-bp) % SUBLANE
    c_all = jnp.concatenate([c_prompt, jnp.zeros((pad, d), F32), c_sample], axis=0)
    mod = _ada(c_all, w_ada, b_ada)
    mod_p = mod[:, :bp]
    mod_s = mod[:, bp + pad:]

    tn_in = 768
    padl = lambda a: jnp.pad(a, ((0, 0), (0, LANE - a.shape[1])))
    weights = (
        _pack_w_in(cfg, w_in, tn_in),
        jnp.pad(w_gla_gate, ((0, 0), (0, LANE - cfg.gla_rank), (0, 0))).astype(BF16),
        b_gla_gate.reshape(depth, 1, H * dk),
        gla_norm.reshape(depth, 1, dv),
        w_gla_proj.astype(BF16),
        conv_w,
        conv_b.reshape(depth, 1, cfg.conv_dim),
        padl(dt_bias).reshape(depth, 1, LANE),
        padl(A_log).reshape(depth, 1, LANE),
        jnp.repeat(d_skip, P, axis=1).reshape(depth, 1, cfg.inner),
        ssd_norm.reshape(depth, 1, cfg.inner),
        w_ssd_proj.astype(BF16),
        w_mix_out.astype(BF16),
        w_ffn_in.astype(BF16),
        w_ffn_out.astype(BF16),
        final_norm,
        tn_in,
    )
    rows_p = _Rows(bp, tp, True)
    rows_s = _Rows(bs, ts, False)
    y_p, gla_p, ssm_p, conv_p = _trunk(cfg, x_prompt.reshape(bp * tp, d), mod_p, rows_p,
                                       (None, None, None), weights, BF16)
    states_s = (state_gla, state_ssm, state_conv)
    y_s, gla_s, ssm_s, conv_s = _trunk(cfg, x_sample.reshape(bs * ts, d), mod_s, rows_s,
                                       states_s, weights, F32)
    return (y_p.reshape(bp, tp, d), y_s.reshape(bs, ts, d), gla_p, ssm_p, conv_p, gla_s, ssm_s, conv_s)


def kernel(x_prompt, x_sample, c_prompt, c_sample, state_gla, state_ssm, state_conv, w_ada, b_ada, w_in,
           w_gla_gate, b_gla_gate, gla_norm, w_gla_proj, conv_w, conv_b, dt_bias, A_log, d_skip, ssd_norm,
           w_ssd_proj, w_mix_out, w_ffn_in, w_ffn_out, final_norm):
    cfg = Cfg()
    assert x_prompt.shape[2] == cfg.d_model
    return _forward(cfg, x_prompt, x_sample, c_prompt, c_sample, state_gla, state_ssm, state_conv,
                    w_ada, b_ada, w_in, w_gla_gate, b_gla_gate, gla_norm, w_gla_proj, conv_w, conv_b,
                    dt_bias, A_log, d_skip, ssd_norm, w_ssd_proj, w_mix_out, w_ffn_in, w_ffn_out, final_norm)
```

```python
import dataclasses
import functools

import jax
import jax.numpy as jnp
from jax import lax
from jax.experimental import pallas as pl
from jax.experimental.pallas import tpu as pltpu

F32 = jnp.float32
BF16 = jnp.bfloat16

LANE = 128
SUBLANE = 8
VMEM_LIMIT = 56 * 1024 * 1024
EPS = 1e-6


@dataclasses.dataclass(frozen=True)
class Cfg:
    d_model: int = 2048
    depth: int = 2
    gla_heads: int = 4
    gla_rank: int = 16
    gla_tau: float = 16.0
    gla_chunk: int = 64
    ssd_head_dim: int = 64
    ssd_groups: int = 8
    ssd_state: int = 128
    ssd_conv: int = 4
    ssd_chunk: int = 64

    @property
    def dk(self):
        return self.d_model // 2 // self.gla_heads

    @property
    def dv(self):
        return self.d_model // self.gla_heads

    @property
    def inner(self):
        return 2 * self.d_model

    @property
    def ssd_heads(self):
        return self.inner // self.ssd_head_dim

    @property
    def hpg(self):
        return self.ssd_heads // self.ssd_groups

    @property
    def gw(self):
        return self.inner // self.ssd_groups

    @property
    def conv_dim(self):
        return self.inner + 2 * self.ssd_groups * self.ssd_state

    @property
    def d_ff(self):
        return ((8 * self.d_model // 3 + 255) // 256) * 256

    @property
    def off_z(self):
        return 0

    @property
    def off_xbc(self):
        return self.inner

    @property
    def off_q(self):
        return self.off_xbc + self.conv_dim

    @property
    def off_k(self):
        return self.off_q + self.gla_heads * self.dk

    @property
    def off_v(self):
        return self.off_k + self.gla_heads * self.dk

    @property
    def off_r(self):
        return self.off_v + self.gla_heads * self.dv

    @property
    def off_gates(self):
        return self.off_r + self.gla_heads * self.dv

    @property
    def off_glr(self):
        return self.off_gates + 2 * self.d_model

    @property
    def off_dt(self):
        return self.off_glr + LANE

    def n_packed(self, tn):
        n = self.off_dt + LANE
        return -(-n // tn) * tn


def _cparams(sem):
    return pltpu.CompilerParams(dimension_semantics=sem, vmem_limit_bytes=VMEM_LIMIT)


def _sigmoid(x):
    return 1.0 / (1.0 + jnp.exp(-x))


def _silu(x):
    hx = 0.5 * x
    return hx + hx * jnp.tanh(hx)


def _softplus(x):
    return jnp.maximum(x, 0.0) + jnp.log1p(jnp.exp(-jnp.abs(x)))


def _log_sigmoid(x):
    return jnp.minimum(x, 0.0) - jnp.log(1.0 + jnp.exp(-jnp.abs(x)))


def _dot(a, b):
    return jnp.dot(a, b, preferred_element_type=F32)


def _dot_nt(a, b):
    return lax.dot_general(a, b, (((1,), (1,)), ((), ())), preferred_element_type=F32)


def _dot_tn(a, b):
    return lax.dot_general(a, b, (((0,), (0,)), ((), ())), preferred_element_type=F32)


def _split(x, parts):
    out = []
    r = x
    for i in range(parts):
        p = r.astype(BF16)
        out.append(p)
        if i + 1 < parts:
            r = r - p.astype(F32)
    return out


def _split_rows(x):
    hi = x.astype(BF16).astype(F32)
    return jnp.concatenate([hi, x - hi], axis=0).astype(BF16)


def _split_lanes(x):
    hi = x.astype(BF16)
    return jnp.concatenate([hi, (x - hi.astype(F32)).astype(BF16)], axis=1)


def _rms(x):
    return x * lax.rsqrt(jnp.mean(x * x, axis=-1, keepdims=True) + EPS)


def _expand_rows(v, reps):
    if reps == 1:
        return v
    g = v.shape[0]
    r = lax.broadcasted_iota(jnp.int32, (g * reps, g), 0)
    c = lax.broadcasted_iota(jnp.int32, (g * reps, g), 1)
    lo = c * reps
    e = jnp.where((r >= lo) & (r < lo + reps), 1.0, 0.0).astype(BF16)
    acc = None
    for p in _split(v, 3):
        d = _dot(e, p)
        acc = d if acc is None else acc + d
    return acc


def _tri(n):
    r = lax.broadcasted_iota(jnp.int32, (n, n), 0)
    c = lax.broadcasted_iota(jnp.int32, (n, n), 1)
    return r >= c


def _ada_kernel(c_ref, w_ref, b_ref, o_ref):
    a = _silu(c_ref[...]).astype(BF16)
    o_ref[...] = _dot(a, w_ref[...].astype(BF16)) + b_ref[...]


NORM_ROWS = 256


def _modulated_norm(x_ref, sc_ref, sh_ref, h_scr, reps):
    @pl.when(pl.program_id(1) == 0)
    def _():
        tm = x_ref.shape[0]
        step = min(NORM_ROWS, tm)
        for r in range(0, tm, step):
            rs = slice(r, r + step)
            if reps == 1:
                sc, sh = sc_ref[...], sh_ref[...]
            else:
                ms = slice(r // reps, (r + step) // reps)
                sc = _expand_rows(sc_ref[ms, :], reps)
                sh = _expand_rows(sh_ref[ms, :], reps)
            h_scr[rs, :] = (_rms(x_ref[rs, :]) * (1.0 + sc) + sh).astype(h_scr.dtype)


def _norm_proj_kernel(x_ref, sc_ref, sh_ref, w_ref, o_ref, h_scr, *, reps):
    _modulated_norm(x_ref, sc_ref, sh_ref, h_scr, reps)
    o_ref[...] = _dot_nt(h_scr[...], w_ref[...])


def _norm_ffn_in_kernel(x_ref, sc_ref, sh_ref, wg_ref, wu_ref, o_ref, h_scr, *, reps):
    _modulated_norm(x_ref, sc_ref, sh_ref, h_scr, reps)
    h = h_scr[...]
    o_ref[...] = (_silu(_dot(h, wg_ref[...])) * _dot(h, wu_ref[...])).astype(o_ref.dtype)


def _final_norm_kernel(x_ref, g_ref, o_ref):
    o_ref[...] = _rms(x_ref[...]) * g_ref[...]


def _merge_kernel(oa_ref, yb_ref, wa_ref, wb_ref, ga_ref, gb_ref, o_ref):
    ya = _dot(oa_ref[...].astype(BF16), wa_ref[...])
    yb = _dot(yb_ref[...].astype(BF16), wb_ref[...])
    o_ref[...] = (_sigmoid(ga_ref[...]) * ya + _sigmoid(gb_ref[...]) * yb).astype(o_ref.dtype)


def _resid_kernel(a_ref, w_ref, x_ref, g_ref, o_ref, *, reps):
    y = _dot(a_ref[...], w_ref[...])
    o_ref[...] = x_ref[...] + _expand_rows(g_ref[...], reps) * y


def _gla_kernel(*refs, L, nc, nt, ns, H, zero_init, has_prev, scale, inv_tau):
    q_ref, k_ref, v_ref, glr_ref, r_ref, wg_ref, bg_ref, gn_ref = refs[:8]
    i = 8
    s0_ref = None
    if not zero_init:
        s0_ref = refs[i]
        i += 1
    if has_prev:
        i += 1
    o_ref, s_out_ref, s_scr = refs[i:i + 3]
    t = pl.program_id(1)
    _, dk, dv = s_scr.shape

    @pl.when(t == 0)
    def _init():
        if zero_init:
            s_scr[...] = jnp.zeros_like(s_scr)
        else:
            s_scr[...] = s0_ref[...].reshape(s_scr.shape)

    ga = _dot(glr_ref[...].astype(BF16), wg_ref[...]) + bg_ref[...]
    log_a = _log_sigmoid(ga) * inv_tau
    tri = jnp.where(_tri(L), 1.0, 0.0).astype(BF16)
    tri2 = jnp.concatenate([tri, tri], axis=1)
    mask = _tri(L)
    ones2 = jnp.ones((2 * L, LANE), BF16)
    heads = range(H)
    ck = [slice(h * dk, (h + 1) * dk) for h in heads]
    cv = [slice(h * dv, (h + 1) * dv) for h in heads]
    for unit in range(ns * nc):
        sl = slice(unit * L, (unit + 1) * L)
        s0 = (unit // nc) * H
        g2 = _split_rows(log_a[sl, :])
        b = _dot(tri2, g2)
        tot = [_dot_tn(g2[:, ck[h]], ones2) for h in heads]
        eb = jnp.exp(b)
        q_in = (q_ref[sl, :] * scale * eb).astype(BF16)
        k = k_ref[sl, :]
        k_in = (k * jnp.exp(-b)).astype(BF16)
        k_out = (k * jnp.exp(b[L - 1:L, :] - b)).astype(BF16)
        v = [v_ref[sl, cv[h]].astype(BF16) for h in heads]
        att = [jnp.where(mask, _dot_nt(q_in[:, ck[h]], k_in[:, ck[h]]), 0.0).astype(BF16) for h in heads]
        s = [s_scr[s0 + h] for h in heads]
        o = [_dot(q_in[:, ck[h]], s[h].astype(BF16)) + _dot(att[h], v[h]) for h in heads]
        upd = [_dot_tn(k_out[:, ck[h]], v[h]) for h in heads]
        for h in heads:
            decay = jnp.exp(tot[h])
            s_scr[s0 + h] = s[h] * jnp.concatenate([decay] * (dv // LANE), axis=1) + upd[h]
            on = _rms(o[h]) * gn_ref[...]
            o_ref[sl, cv[h]] = (on * _silu(r_ref[sl, cv[h]])).astype(o_ref.dtype)

    @pl.when(t == nt - 1)
    def _fin():
        s_out_ref[...] = s_scr[...].reshape(s_out_ref.shape)


def _ssd_kernel(*refs, L, nc, nt, G, hpg, P, N, zero_init, has_prev, has_exl):
    (z_ref, x_ref, b_ref, c_ref, dt_ref, cw_ref, cb_ref, dtb_ref, alog_ref, dsk_ref, nrm_ref, exp_ref) = refs[:12]
    i = 12
    exl_ref = exp_ref
    if has_exl:
        exl_ref = refs[i]
        i += 1
    cs_ref = h0_ref = None
    if not zero_init:
        cs_ref, h0_ref = refs[i:i + 2]
        i += 2
    if has_prev:
        i += 1
    y_ref, h_out_ref, u, h_scr, xbd = refs[i:i + 5]
    t = pl.program_id(1)
    Tb = L * nc
    gw = hpg * P
    inner = G * gw
    wl = hpg * L
    nconv = cw_ref.shape[0]
    halo = SUBLANE

    direct = nt == 1 and nc == 1

    @pl.when(t == 0)
    def _init():
        xbd[...] = jnp.zeros_like(xbd)
        u[0:halo, :] = jnp.zeros((halo, u.shape[1]), F32)
        if not zero_init:
            u[halo - (nconv - 1):halo, :] = cs_ref[...]
        if not direct:
            if zero_init:
                h_scr[...] = jnp.zeros_like(h_scr)
            else:
                h_scr[...] = h0_ref[...].reshape(inner, N)

    def load_h(g):
        if not direct:
            return h_scr[g * gw:(g + 1) * gw, :]
        if zero_init:
            return jnp.zeros((gw, N), F32)
        return h0_ref[g * hpg:(g + 1) * hpg].reshape(gw, N)

    def store_h(g, val):
        if direct:
            h_out_ref[g * hpg:(g + 1) * hpg] = val.reshape(hpg, P, N)
        else:
            h_scr[g * gw:(g + 1) * gw, :] = val

    u[halo:halo + Tb, 0:inner] = x_ref[...]
    u[halo:halo + Tb, inner:inner + G * N] = b_ref[...]
    u[halo:halo + Tb, inner + G * N:inner + 2 * G * N] = c_ref[...]

    def conv(r0, lo, width):
        win = u[pl.ds(r0, L + halo), lo:lo + width]
        acc = cb_ref[:, lo:lo + width] + win[halo:, :] * cw_ref[nconv - 1:nconv, lo:lo + width]
        for j in range(nconv - 1):
            moved = pltpu.roll(win, nconv - 1 - j, 0)[halo:, :]
            acc = acc + moved * cw_ref[j:j + 1, lo:lo + width]
        return _silu(acc)

    dtp = _softplus(dt_ref[...] + dtb_ref[...])
    a_h = dtp * (-jnp.exp(alog_ref[...]))

    tri = jnp.where(_tri(L), 1.0, 0.0).astype(BF16)
    tri2 = jnp.concatenate([tri, tri], axis=1)
    ones2 = jnp.ones((2 * L, LANE), BF16)
    row_l = lax.broadcasted_iota(jnp.int32, (L, G * wl), 0)
    s_l = lax.broadcasted_iota(jnp.int32, (L, G * wl), 1) % L
    later = row_l > s_l
    causal = row_l >= s_l
    early_cast = L % 16 == 0
    lane_in_piece = lax.broadcasted_iota(jnp.int32, (L, LANE), 1)
    gc = [slice(g * gw, (g + 1) * gw) for g in range(G)]
    gbatch = 2 if L >= P else G
    assert G % gbatch == 0

    for c in range(nc):
        sl = slice(c * L, (c + 1) * L)
        da2 = _split_lanes(jnp.concatenate([dtp[sl], a_h[sl]], axis=0))
        da_exp = _dot(da2, exp_ref[...])
        dt_exp = da_exp[:L]
        a_exp_p = da_exp[L:]
        a_exp_l = _dot(_split_lanes(a_h[sl]), exl_ref[...]) if has_exl else a_exp_p
        ap2 = _split_rows(a_exp_p)
        al2 = _split_rows(jnp.where(later, a_exp_l, 0.0))
        ccol = _dot(tri2, ap2)
        seg = _dot(tri2, al2)
        decay = jnp.where(causal, jnp.exp(seg), 0.0)
        grow = jnp.exp(ccol)
        fade = jnp.exp(ccol[L - 1:L, :] - ccol)
        lp = gc
        ll = [slice(g * wl, (g + 1) * wl) for g in range(G)]

        for g0 in range(0, G, gbatch):
            batch = range(g0, g0 + gbatch)
            xs, bs, w_cats, inters = {}, {}, {}, {}
            for g in batch:
                x = conv(c * L, g * gw, gw)
                b = conv(c * L, inner + g * N, N)
                cm = conv(c * L, inner + G * N + g * N, N).astype(BF16)
                if early_cast:
                    b_t = jnp.concatenate([b.astype(BF16)] * hpg, axis=0)
                else:
                    b_t = jnp.concatenate([b] * hpg, axis=0).astype(BF16)
                w_cats[g] = (_dot_nt(cm, b_t) * decay[:, ll[g]]).astype(BF16)
                inters[g] = _dot_nt(cm, load_h(g).astype(BF16))
                xs[g] = x
                bs[g] = b

            xts, intras = {}, {}
            for g in batch:
                xt = xs[g] * dt_exp[:, lp[g]]
                xt_c = xt.astype(xbd.dtype)
                slab = g * nc + c
                for hh in range(hpg):
                    lo = (hh * P) // LANE * LANE
                    piece = xt_c[:, lo:lo + LANE]
                    own = (lane_in_piece >= hh * P - lo) & (lane_in_piece < (hh + 1) * P - lo)
                    xbd[slab, hh * L:(hh + 1) * L, lo:lo + LANE] = jnp.where(own, piece, jnp.zeros_like(piece))
                intras[g] = _dot(w_cats[g], xbd[slab].astype(BF16))
                xts[g] = xt

            for g in batch:
                x_dl = (xts[g] * fade[:, lp[g]]).astype(BF16)
                tot = _dot_tn(ap2[:, lp[g]], ones2)
                h = load_h(g)
                store_h(g, h * jnp.concatenate([jnp.exp(tot)] * (N // LANE), axis=1)
                        + _dot_tn(x_dl, bs[g].astype(BF16)))

            for g in batch:
                y = intras[g] + grow[:, lp[g]] * inters[g] + dsk_ref[:, gc[g]] * xs[g]
                yg = y * _silu(z_ref[sl, gc[g]])
                y_ref[sl, gc[g]] = (_rms(yg) * nrm_ref[:, gc[g]]).astype(y_ref.dtype)

    u[0:halo, :] = u[Tb:Tb + halo, :]

    if not direct:
        @pl.when(t == nt - 1)
        def _fin():
            h_out_ref[...] = h_scr[...].reshape(h_out_ref.shape)


def _ada(c_all, w_ada, b_ada, tn=1024):
    depth, d, n = w_ada.shape
    m = c_all.shape[0]
    return pl.pallas_call(
        _ada_kernel,
        grid=(depth, n // tn),
        in_specs=[
            pl.BlockSpec((m, d), lambda l, j: (0, 0)),
            pl.BlockSpec((None, d, tn), lambda l, j: (l, 0, j)),
            pl.BlockSpec((None, 1, tn), lambda l, j: (l, 0, j)),
        ],
        out_specs=pl.BlockSpec((None, m, tn), lambda l, j: (l, 0, j)),
        out_shape=jax.ShapeDtypeStruct((depth, m, n), F32),
        compiler_params=_cparams(("parallel", "parallel")),
        name="ada",
    )(c_all, w_ada, b_ada.reshape(depth, 1, n))


class _Rows:
    def __init__(self, n_seq, seq_len, long_seq, tm_max=1024):
        self.n_seq, self.seq_len, self.long_seq = n_seq, seq_len, long_seq
        self.m = n_seq * seq_len
        if long_seq:
            self.tm = min(tm_max, seq_len)
            assert seq_len % self.tm == 0
            self.reps = 1
            self.mod_rows = 1
        else:
            self.tm = min(self.m, tm_max)
            assert self.m % self.tm == 0 and self.tm % seq_len == 0
            self.reps = seq_len
            self.mod_rows = self.tm // seq_len
        self.n_tiles = self.m // self.tm

    def lhs_spec(self, width):
        if self.n_tiles == 1:
            return pl.BlockSpec((self.tm, width), lambda i, j: (i, 0), pipeline_mode=pl.Buffered(1))
        return pl.BlockSpec((self.tm, width), lambda i, j: (i, 0))

    def mod_array(self, mod):
        depth, _, n = mod.shape
        if self.reps == 1:
            return mod.reshape(depth, self.n_seq, 1, n)
        return mod

    def mod_spec(self, l, tn, col_block):
        if self.reps == 1:
            per = self.seq_len // self.tm
            return pl.BlockSpec((None, None, 1, tn), lambda i, j: (l, i // per, 0, col_block(j)))
        return pl.BlockSpec((None, self.mod_rows, tn), lambda i, j: (l, i, col_block(j)))


def _final_norm(x, gain, rows):
    m, d = x.shape
    return pl.pallas_call(
        _final_norm_kernel,
        grid=(rows.n_tiles,),
        in_specs=[pl.BlockSpec((rows.tm, d), lambda i: (i, 0)), pl.BlockSpec((1, d), lambda i: (0, 0))],
        out_specs=pl.BlockSpec((rows.tm, d), lambda i: (i, 0)),
        out_shape=jax.ShapeDtypeStruct((m, d), F32),
        compiler_params=_cparams(("parallel",)),
        name="final_norm",
    )(x, gain.reshape(1, d))


def _in_proj(x, mod_arr, w_t, l, rows, tn, v_scale, v_shift):
    m, d = x.shape
    n = w_t.shape[1]
    return pl.pallas_call(
        functools.partial(_norm_proj_kernel, reps=rows.reps),
        grid=(rows.n_tiles, n // tn),
        in_specs=[
            rows.lhs_spec(d),
            rows.mod_spec(l, d, lambda j: v_scale),
            rows.mod_spec(l, d, lambda j: v_shift),
            pl.BlockSpec((None, tn, d), lambda i, j: (l, j, 0)),
        ],
        out_specs=pl.BlockSpec((rows.tm, tn), lambda i, j: (i, j)),
        out_shape=jax.ShapeDtypeStruct((m, n), F32),
        scratch_shapes=[pltpu.VMEM((rows.tm, d), BF16)],
        compiler_params=_cparams(("parallel", "arbitrary")),
        name="in_proj",
    )(x, mod_arr, mod_arr, w_t)


def _merge(cfg, oa, yb, proj, wa, wb, l, rows):
    m = oa.shape[0]
    d = cfg.d_model
    tn = 512 if oa.dtype == BF16 else 256
    ga0 = cfg.off_gates // tn
    gb0 = (cfg.off_gates + d) // tn
    return pl.pallas_call(
        _merge_kernel,
        grid=(rows.n_tiles, d // tn),
        in_specs=[
            rows.lhs_spec(oa.shape[1]),
            rows.lhs_spec(yb.shape[1]),
            pl.BlockSpec((None, wa.shape[1], tn), lambda i, j: (l, 0, j)),
            pl.BlockSpec((None, wb.shape[1], tn), lambda i, j: (l, 0, j)),
            pl.BlockSpec((rows.tm, tn), lambda i, j: (i, ga0 + j)),
            pl.BlockSpec((rows.tm, tn), lambda i, j: (i, gb0 + j)),
        ],
        out_specs=pl.BlockSpec((rows.tm, tn), lambda i, j: (i, j)),
        out_shape=jax.ShapeDtypeStruct((m, d), BF16),
        compiler_params=_cparams(("parallel", "arbitrary")),
        name="merge",
    )(oa, yb, wa, wb, proj, proj)


def _resid(a, w, x, mod_arr, rows, l, v_gate, tn=512, name="resid"):
    m, kdim = a.shape
    d = x.shape[1]
    per = d // tn
    return pl.pallas_call(
        functools.partial(_resid_kernel, reps=rows.reps),
        grid=(rows.n_tiles, d // tn),
        in_specs=[
            rows.lhs_spec(kdim),
            pl.BlockSpec((None, kdim, tn), lambda i, j: (l, 0, j)),
            pl.BlockSpec((rows.tm, tn), lambda i, j: (i, j)),
            rows.mod_spec(l, tn, lambda j: v_gate * per + j),
        ],
        out_specs=pl.BlockSpec((rows.tm, tn), lambda i, j: (i, j)),
        out_shape=jax.ShapeDtypeStruct((m, d), F32),
        compiler_params=_cparams(("parallel", "arbitrary")),
        name=name,
    )(a, w, x, mod_arr)


def _ffn_in(x, mod_arr, w, l, rows, d_ff, v_scale, v_shift, tn=512):
    m, d = x.shape
    nb = d_ff // tn
    return pl.pallas_call(
        functools.partial(_norm_ffn_in_kernel, reps=rows.reps),
        grid=(rows.n_tiles, nb),
        in_specs=[
            rows.lhs_spec(d),
            rows.mod_spec(l, d, lambda j: v_scale),
            rows.mod_spec(l, d, lambda j: v_shift),
            pl.BlockSpec((None, d, tn), lambda i, j: (l, 0, j)),
            pl.BlockSpec((None, d, tn), lambda i, j: (l, 0, nb + j)),
        ],
        out_specs=pl.BlockSpec((rows.tm, tn), lambda i, j: (i, j)),
        out_shape=jax.ShapeDtypeStruct((m, d_ff), BF16),
        scratch_shapes=[pltpu.VMEM((rows.tm, d), BF16)],
        compiler_params=_cparams(("parallel", "arbitrary")),
        name="ffn_in",
    )(x, mod_arr, mod_arr, w, w)


def _scan_blocks(rows, chunk, max_chunks):
    L = min(chunk, rows.seq_len)
    assert rows.seq_len % L == 0
    n_chunks = rows.seq_len // L
    nc = min(max_chunks, n_chunks)
    assert n_chunks % nc == 0
    return L, nc, n_chunks // nc


def _gla(cfg, proj, l, rows, wg, bg, gn, state, prev, out_dtype):
    H, dk, dv = cfg.gla_heads, cfg.dk, cfg.dv
    hk, hv = H * dk, H * dv
    B = rows.n_seq
    L, nc, nt = _scan_blocks(rows, cfg.gla_chunk, 4)
    ns = 1
    if nt == 1 and nc == 1:
        ns = 4 if B % 4 == 0 else (2 if B % 2 == 0 else 1)
    Tb = ns * L * nc
    assert cfg.off_q % hk == 0 and cfg.off_v % hv == 0
    rb = lambda b, t: b * nt + t
    in_specs = [
        pl.BlockSpec((Tb, hk), lambda b, t: (rb(b, t), cfg.off_q // hk)),
        pl.BlockSpec((Tb, hk), lambda b, t: (rb(b, t), cfg.off_k // hk)),
        pl.BlockSpec((Tb, hv), lambda b, t: (rb(b, t), cfg.off_v // hv)),
        pl.BlockSpec((Tb, LANE), lambda b, t: (rb(b, t), cfg.off_glr // LANE)),
        pl.BlockSpec((Tb, hv), lambda b, t: (rb(b, t), cfg.off_r // hv)),
        pl.BlockSpec((None, LANE, hk), lambda b, t: (l, 0, 0)),
        pl.BlockSpec((None, 1, hk), lambda b, t: (l, 0, 0)),
        pl.BlockSpec((None, 1, dv), lambda b, t: (l, 0, 0)),
    ]
    args = [proj, proj, proj, proj, proj, wg, bg, gn]
    state_spec = pl.BlockSpec((None, ns, H, dk, dv), lambda b, t: (l, b, 0, 0, 0))
    if state is not None:
        in_specs.append(state_spec)
        args.append(state)
    aliases = {}
    if prev is not None:
        in_specs.append(pl.BlockSpec(memory_space=pl.ANY))
        aliases = {len(args): 1}
        args.append(prev)
    kern = functools.partial(_gla_kernel, L=L, nc=nc, nt=nt, ns=ns, H=H, zero_init=state is None,
                             has_prev=prev is not None, scale=dk ** -0.5, inv_tau=1.0 / cfg.gla_tau)
    return pl.pallas_call(
        kern,
        grid=(B // ns, nt),
        in_specs=in_specs,
        out_specs=[pl.BlockSpec((Tb, hv), lambda b, t: (rb(b, t), 0)), state_spec],
        out_shape=[
            jax.ShapeDtypeStruct((rows.m, hv), out_dtype),
            jax.ShapeDtypeStruct((cfg.depth, B, H, dk, dv), F32),
        ],
        scratch_shapes=[pltpu.VMEM((ns * H, dk, dv), F32)],
        input_output_aliases=aliases,
        compiler_params=_cparams(("parallel", "arbitrary")),
        name="gla_scan",
    )(*args)


def _head_expander(n_heads, per):
    col_head = jnp.arange(n_heads * per, dtype=jnp.int32) // per
    ex = (col_head[None, :] == jnp.arange(LANE, dtype=jnp.int32)[:, None]).astype(BF16)
    return jnp.concatenate([ex, ex], axis=0)


def _ssd(cfg, proj, l, rows, conv_w, conv_b, dtb, alog, dsk, nrm, conv_state, state, prev, out_dtype):
    G, gw, N, P, hpg = cfg.ssd_groups, cfg.gw, cfg.ssd_state, cfg.ssd_head_dim, cfg.hpg
    inner, heads, cdim = cfg.inner, cfg.ssd_heads, cfg.conv_dim
    B = rows.n_seq
    K = cfg.ssd_conv
    L, nc, nt = _scan_blocks(rows, cfg.ssd_chunk, 2)
    Tb = L * nc
    wl = hpg * L
    gn = G * N
    assert cfg.off_z % inner == 0 and cfg.off_xbc % inner == 0 and (cfg.off_xbc + inner) % gn == 0
    rb = lambda b, t: b * nt + t
    in_specs = [
        pl.BlockSpec((Tb, inner), lambda b, t: (rb(b, t), cfg.off_z // inner)),
        pl.BlockSpec((Tb, inner), lambda b, t: (rb(b, t), cfg.off_xbc // inner)),
        pl.BlockSpec((Tb, gn), lambda b, t: (rb(b, t), (cfg.off_xbc + inner) // gn)),
        pl.BlockSpec((Tb, gn), lambda b, t: (rb(b, t), (cfg.off_xbc + inner) // gn + 1)),
        pl.BlockSpec((Tb, LANE), lambda b, t: (rb(b, t), cfg.off_dt // LANE)),
        pl.BlockSpec((None, K, cdim), lambda b, t: (l, 0, 0)),
        pl.BlockSpec((None, 1, cdim), lambda b, t: (l, 0, 0)),
        pl.BlockSpec((None, 1, LANE), lambda b, t: (l, 0, 0)),
        pl.BlockSpec((None, 1, LANE), lambda b, t: (l, 0, 0)),
        pl.BlockSpec((None, 1, inner), lambda b, t: (l, 0, 0)),
        pl.BlockSpec((None, 1, inner), lambda b, t: (l, 0, 0)),
        pl.BlockSpec((2 * LANE, inner), lambda b, t: (0, 0)),
    ]
    args = [proj, proj, proj, proj, proj, conv_w, conv_b, dtb, alog, dsk, nrm, _head_expander(heads, P)]
    has_exl = L != P
    if has_exl:
        in_specs.append(pl.BlockSpec((2 * LANE, heads * L), lambda b, t: (0, 0)))
        args.append(_head_expander(heads, L))
    state_spec = pl.BlockSpec((None, None, heads, P, N), lambda b, t: (l, b, 0, 0, 0))
    if state is not None:
        in_specs += [pl.BlockSpec((None, None, K - 1, cdim), lambda b, t: (l, b, 0, 0)), state_spec]
        args += [conv_state, state]
    aliases = {}
    if prev is not None:
        in_specs.append(pl.BlockSpec(memory_space=pl.ANY))
        aliases = {len(args): 1}
        args.append(prev)
    kern = functools.partial(_ssd_kernel, L=L, nc=nc, nt=nt, G=G, hpg=hpg, P=P, N=N, zero_init=state is None,
                             has_prev=prev is not None, has_exl=has_exl)
    xbd_dtype = BF16 if L % 16 == 0 else F32
    return pl.pallas_call(
        kern,
        grid=(B, nt),
        in_specs=in_specs,
        out_specs=[pl.BlockSpec((Tb, inner), lambda b, t: (rb(b, t), 0)), state_spec],
        out_shape=[
            jax.ShapeDtypeStruct((rows.m, inner), out_dtype),
            jax.ShapeDtypeStruct((cfg.depth, B, heads, P, N), F32),
        ],
        scratch_shapes=[
            pltpu.VMEM((Tb + SUBLANE, cdim), F32),
            pltpu.VMEM((inner if nt * nc > 1 else SUBLANE, N), F32),
            pltpu.VMEM((G * nc, wl, gw), xbd_dtype),
        ],
        input_output_aliases=aliases,
        compiler_params=_cparams(("parallel", "arbitrary")),
        name="ssd_scan",
    )(*args)


def _pack_w_in(cfg, w_in, tn):
    H, dk, dv = cfg.gla_heads, cfg.dk, cfg.dv
    sizes = (H * dk, H * dk, H * dv, H * dv, cfg.gla_rank, cfg.inner, cfg.conv_dim, cfg.ssd_heads, 2 * cfg.d_model)
    offs = [0]
    for s in sizes:
        offs.append(offs[-1] + s)
    w_t = jnp.swapaxes(w_in, 1, 2)
    seg = lambda i: w_t[:, offs[i]:offs[i + 1], :].astype(BF16)
    padto = lambda a, n: jnp.pad(a, ((0, 0), (0, n - a.shape[1]), (0, 0)))
    n_p = cfg.n_packed(tn)
    parts = [seg(5), seg(6), seg(0), seg(1), seg(2), seg(3), seg(8), padto(seg(4), LANE), padto(seg(7), LANE)]
    return padto(jnp.concatenate(parts, axis=1), n_p)


def _trunk(cfg, x, mod, rows, states, weights, scan_dtype):
    (w_in_p, wg, bg, gn, w_gla_proj, conv_w, conv_b, dtb, alog, dsk, nrm, w_ssd_proj, w_mix_out,
     w_ffn_in, w_ffn_out, final_norm, tn_in) = weights
    s_gla, s_ssm, s_conv = states
    mod_arr = rows.mod_array(mod)
    rows_e = _Rows(rows.n_seq, rows.seq_len, rows.long_seq, tm_max=512)
    B = rows.n_seq
    K = cfg.ssd_conv
    new_gla = new_ssm = None
    new_conv = []
    for l in range(cfg.depth):
        proj = _in_proj(x, mod_arr, w_in_p, l, rows, tn_in, 1, 0)
        oa, new_gla = _gla(cfg, proj, l, rows, wg, bg, gn, s_gla, new_gla, scan_dtype)
        yb, new_ssm = _ssd(cfg, proj, l, rows, conv_w, conv_b, dtb, alog, dsk, nrm, s_conv, s_ssm, new_ssm,
                           scan_dtype)
        assert rows.seq_len >= K - 1
        tail = proj.reshape(B, rows.seq_len, proj.shape[1])[:, rows.seq_len - (K - 1):, :]
        new_conv.append(tail[:, :, cfg.off_xbc:cfg.off_xbc + cfg.conv_dim])
        merged = _merge(cfg, oa, yb, proj, w_gla_proj, w_ssd_proj, l, rows)
        x = _resid(merged, w_mix_out, x, mod_arr, rows, l, 2, tn=1024, name="mix_out")
        act = _ffn_in(x, mod_arr, w_ffn_in, l, rows, cfg.d_ff, 4, 3)
        x = _resid(act, w_ffn_out, x, mod_arr, rows, l, 5, name="ffn_out")
    y = _final_norm(x, final_norm, rows_e)
    return y, new_gla, new_ssm, jnp.stack(new_conv)


def _forward(cfg, x_prompt, x_sample, c_prompt, c_sample, state_gla, state_ssm, state_conv,
             w_ada, b_ada, w_in, w_gla_gate, b_gla_gate, gla_norm, w_gla_proj, conv_w, conv_b,
             dt_bias, A_log, d_skip, ssd_norm, w_ssd_proj, w_mix_out, w_ffn_in, w_ffn_out, final_norm):
    depth, d = cfg.depth, cfg.d_model
    bp, tp, _ = x_prompt.shape
    bs, ts, _ = x_sample.shape
    H, dk, dv = cfg.gla_heads, cfg.dk, cfg.dv
    P = cfg.ssd_head_dim
    assert cfg.ssd_heads <= LANE and cfg.gla_rank <= LANE
    assert cfg.off_gates % 256 == 0

    pad = (-bp) % SUBLANE
    c_all = jnp.concatenate([c_prompt, jnp.zeros((pad, d), F32), c_sample], axis=0)
    mod = _ada(c_all, w_ada, b_ada)
    mod_p = mod[:, :bp]
    mod_s = mod[:, bp + pad:]

    tn_in = 768
    padl = lambda a: jnp.pad(a, ((0, 0), (0, LANE - a.shape[1])))
    weights = (
        _pack_w_in(cfg, w_in, tn_in),
        jnp.pad(w_gla_gate, ((0, 0), (0, LANE - cfg.gla_rank), (0, 0))).astype(BF16),
        b_gla_gate.reshape(depth, 1, H * dk),
        gla_norm.reshape(depth, 1, dv),
        w_gla_proj.astype(BF16),
        conv_w,
        conv_b.reshape(depth, 1, cfg.conv_dim),
        padl(dt_bias).reshape(depth, 1, LANE),
        padl(A_log).reshape(depth, 1, LANE),
        jnp.repeat(d_skip, P, axis=1).reshape(depth, 1, cfg.inner),
        ssd_norm.reshape(depth, 1, cfg.inner),
        w_ssd_proj.astype(BF16),
        w_mix_out.astype(BF16),
        w_ffn_in.astype(BF16),
        w_ffn_out.astype(BF16),
        final_norm,
        tn_in,
    )
    rows_p = _Rows(bp, tp, True)
    rows_s = _Rows(bs, ts, False)
    y_p, gla_p, ssm_p, conv_p = _trunk(cfg, x_prompt.reshape(bp * tp, d), mod_p, rows_p,
                                       (None, None, None), weights, BF16)
    states_s = (state_gla, state_ssm, state_conv)
    y_s, gla_s, ssm_s, conv_s = _trunk(cfg, x_sample.reshape(bs * ts, d), mod_s, rows_s,
                                       states_s, weights, F32)
    return (y_p.reshape(bp, tp, d), y_s.reshape(bs, ts, d), gla_p, ssm_p, conv_p, gla_s, ssm_s, conv_s)


def kernel(x_prompt, x_sample, c_prompt, c_sample, state_gla, state_ssm, state_conv, w_ada, b_ada, w_in,
           w_gla_gate, b_gla_gate, gla_norm, w_gla_proj, conv_w, conv_b, dt_bias, A_log, d_skip, ssd_norm,
           w_ssd_proj, w_mix_out, w_ffn_in, w_ffn_out, final_norm):
    cfg = Cfg()
    assert x_prompt.shape[2] == cfg.d_model
    return _forward(cfg, x_prompt, x_sample, c_prompt, c_sample, state_gla, state_ssm, state_conv,
                    w_ada, b_ada, w_in, w_gla_gate, b_gla_gate, gla_norm, w_gla_proj, conv_w, conv_b,
                    dt_bias, A_log, d_skip, ssd_norm, w_ssd_proj, w_mix_out, w_ffn_in, w_ffn_out, final_norm)
```

```python
import dataclasses
import functools

import jax
import jax.numpy as jnp
from jax import lax
from jax.experimental import pallas as pl
from jax.experimental.pallas import tpu as pltpu

F32 = jnp.float32
BF16 = jnp.bfloat16

LANE = 128
SUBLANE = 8
VMEM_LIMIT = 56 * 1024 * 1024
EPS = 1e-6


@dataclasses.dataclass(frozen=True)
class Cfg:
    d_model: int = 2048
    depth: int = 2
    gla_heads: int = 4
    gla_rank: int = 16
    gla_tau: float = 16.0
    gla_chunk: int = 64
    ssd_head_dim: int = 64
    ssd_groups: int = 8
    ssd_state: int = 128
    ssd_conv: int = 4
    ssd_chunk: int = 64

    @property
    def dk(self):
        return self.d_model // 2 // self.gla_heads

    @property
    def dv(self):
        return self.d_model // self.gla_heads

    @property
    def inner(self):
        return 2 * self.d_model

    @property
    def ssd_heads(self):
        return self.inner // self.ssd_head_dim

    @property
    def hpg(self):
        return self.ssd_heads // self.ssd_groups

    @property
    def gw(self):
        return self.inner // self.ssd_groups

    @property
    def conv_dim(self):
        return self.inner + 2 * self.ssd_groups * self.ssd_state

    @property
    def d_ff(self):
        return ((8 * self.d_model // 3 + 255) // 256) * 256

    @property
    def off_z(self):
        return 0

    @property
    def off_xbc(self):
        return self.inner

    @property
    def off_q(self):
        return self.off_xbc + self.conv_dim

    @property
    def off_k(self):
        return self.off_q + self.gla_heads * self.dk

    @property
    def off_v(self):
        return self.off_k + self.gla_heads * self.dk

    @property
    def off_r(self):
        return self.off_v + self.gla_heads * self.dv

    @property
    def off_gates(self):
        return self.off_r + self.gla_heads * self.dv

    @property
    def off_glr(self):
        return self.off_gates + 2 * self.d_model

    @property
    def off_dt(self):
        return self.off_glr + LANE

    def n_packed(self, tn):
        n = self.off_dt + LANE
        return -(-n // tn) * tn


def _cparams(sem):
    return pltpu.CompilerParams(dimension_semantics=sem, vmem_limit_bytes=VMEM_LIMIT)


def _sigmoid(x):
    return 1.0 / (1.0 + jnp.exp(-x))


def _silu(x):
    hx = 0.5 * x
    return hx + hx * jnp.tanh(hx)


def _softplus(x):
    return jnp.maximum(x, 0.0) + jnp.log1p(jnp.exp(-jnp.abs(x)))


def _log_sigmoid(x):
    return jnp.minimum(x, 0.0) - jnp.log(1.0 + jnp.exp(-jnp.abs(x)))


def _dot(a, b):
    return jnp.dot(a, b, preferred_element_type=F32)


def _dot_nt(a, b):
    return lax.dot_general(a, b, (((1,), (1,)), ((), ())), preferred_element_type=F32)


def _dot_tn(a, b):
    return lax.dot_general(a, b, (((0,), (0,)), ((), ())), preferred_element_type=F32)


def _split(x, parts):
    out = []
    r = x
    for i in range(parts):
        p = r.astype(BF16)
        out.append(p)
        if i + 1 < parts:
            r = r - p.astype(F32)
    return out


def _split_rows(x):
    hi = x.astype(BF16).astype(F32)
    return jnp.concatenate([hi, x - hi], axis=0).astype(BF16)


def _split_lanes(x):
    hi = x.astype(BF16)
    return jnp.concatenate([hi, (x - hi.astype(F32)).astype(BF16)], axis=1)


def _rms(x):
    return x * lax.rsqrt(jnp.mean(x * x, axis=-1, keepdims=True) + EPS)


def _expand_rows(v, reps):
    if reps == 1:
        return v
    g = v.shape[0]
    r = lax.broadcasted_iota(jnp.int32, (g * reps, g), 0)
    c = lax.broadcasted_iota(jnp.int32, (g * reps, g), 1)
    lo = c * reps
    e = jnp.where((r >= lo) & (r < lo + reps), 1.0, 0.0).astype(BF16)
    acc = None
    for p in _split(v, 3):
        d = _dot(e, p)
        acc = d if acc is None else acc + d
    return acc


def _tri(n):
    r = lax.broadcasted_iota(jnp.int32, (n, n), 0)
    c = lax.broadcasted_iota(jnp.int32, (n, n), 1)
    return r >= c


def _ada_kernel(c_ref, w_ref, b_ref, o_ref):
    a = _silu(c_ref[...]).astype(BF16)
    o_ref[...] = _dot(a, w_ref[...].astype(BF16)) + b_ref[...]


NORM_ROWS = 256


def _modulated_norm(x_ref, sc_ref, sh_ref, h_scr, reps):
    @pl.when(pl.program_id(1) == 0)
    def _():
        tm = x_ref.shape[0]
        step = min(NORM_ROWS, tm)
        for r in range(0, tm, step):
            rs = slice(r, r + step)
            if reps == 1:
                sc, sh = sc_ref[...], sh_ref[...]
            else:
                ms = slice(r // reps, (r + step) // reps)
                sc = _expand_rows(sc_ref[ms, :], reps)
                sh = _expand_rows(sh_ref[ms, :], reps)
            h_scr[rs, :] = (_rms(x_ref[rs, :]) * (1.0 + sc) + sh).astype(h_scr.dtype)


def _norm_proj_kernel(x_ref, sc_ref, sh_ref, w_ref, o_ref, h_scr, *, reps):
    _modulated_norm(x_ref, sc_ref, sh_ref, h_scr, reps)
    o_ref[...] = _dot_nt(h_scr[...], w_ref[...])


def _norm_ffn_in_kernel(x_ref, sc_ref, sh_ref, wg_ref, wu_ref, o_ref, h_scr, *, reps):
    _modulated_norm(x_ref, sc_ref, sh_ref, h_scr, reps)
    h = h_scr[...]
    o_ref[...] = (_silu(_dot(h, wg_ref[...])) * _dot(h, wu_ref[...])).astype(o_ref.dtype)


def _final_norm_kernel(x_ref, g_ref, o_ref):
    o_ref[...] = _rms(x_ref[...]) * g_ref[...]


def _merge_kernel(oa_ref, yb_ref, wa_ref, wb_ref, ga_ref, gb_ref, o_ref):
    ya = _dot(oa_ref[...].astype(BF16), wa_ref[...])
    yb = _dot(yb_ref[...].astype(BF16), wb_ref[...])
    o_ref[...] = (_sigmoid(ga_ref[...]) * ya + _sigmoid(gb_ref[...]) * yb).astype(o_ref.dtype)


def _resid_kernel(a_ref, w_ref, x_ref, g_ref, o_ref, *, reps):
    y = _dot(a_ref[...], w_ref[...])
    o_ref[...] = x_ref[...] + _expand_rows(g_ref[...], reps) * y


def _gla_kernel(*refs, L, nc, nt, ns, H, zero_init, has_prev, scale, inv_tau):
    q_ref, k_ref, v_ref, glr_ref, r_ref, wg_ref, bg_ref, gn_ref = refs[:8]
    i = 8
    s0_ref = None
    if not zero_init:
        s0_ref = refs[i]
        i += 1
    if has_prev:
        i += 1
    o_ref, s_out_ref, s_scr = refs[i:i + 3]
    t = pl.program_id(1)
    _, dk, dv = s_scr.shape

    @pl.when(t == 0)
    def _init():
        if zero_init:
            s_scr[...] = jnp.zeros_like(s_scr)
        else:
            s_scr[...] = s0_ref[...].reshape(s_scr.shape)

    ga = _dot(glr_ref[...].astype(BF16), wg_ref[...]) + bg_ref[...]
    log_a = _log_sigmoid(ga) * inv_tau
    tri = jnp.where(_tri(L), 1.0, 0.0).astype(BF16)
    tri2 = jnp.concatenate([tri, tri], axis=1)
    mask = _tri(L)
    ones2 = jnp.ones((2 * L, LANE), BF16)
    heads = range(H)
    ck = [slice(h * dk, (h + 1) * dk) for h in heads]
    cv = [slice(h * dv, (h + 1) * dv) for h in heads]
    for unit in range(ns * nc):
        sl = slice(unit * L, (unit + 1) * L)
        s0 = (unit // nc) * H
        g2 = _split_rows(log_a[sl, :])
        b = _dot(tri2, g2)
        tot = [_dot_tn(g2[:, ck[h]], ones2) for h in heads]
        eb = jnp.exp(b)
        q_in = (q_ref[sl, :] * scale * eb).astype(BF16)
        k = k_ref[sl, :]
        k_in = (k * jnp.exp(-b)).astype(BF16)
        k_out = (k * jnp.exp(b[L - 1:L, :] - b)).astype(BF16)
        v = [v_ref[sl, cv[h]].astype(BF16) for h in heads]
        att = [jnp.where(mask, _dot_nt(q_in[:, ck[h]], k_in[:, ck[h]]), 0.0).astype(BF16) for h in heads]
        s = [s_scr[s0 + h] for h in heads]
        o = [_dot(q_in[:, ck[h]], s[h].astype(BF16)) + _dot(att[h], v[h]) for h in heads]
        upd = [_dot_tn(k_out[:, ck[h]], v[h]) for h in heads]
        for h in heads:
            decay = jnp.exp(tot[h])
            s_scr[s0 + h] = s[h] * jnp.concatenate([decay] * (dv // LANE), axis=1) + upd[h]
            on = _rms(o[h]) * gn_ref[...]
            o_ref[sl, cv[h]] = (on * _silu(r_ref[sl, cv[h]])).astype(o_ref.dtype)

    @pl.when(t == nt - 1)
    def _fin():
        s_out_ref[...] = s_scr[...].reshape(s_out_ref.shape)


def _ssd_kernel(*refs, L, nc, nt, G, hpg, P, N, zero_init, has_prev, has_exl):
    (z_ref, x_ref, b_ref, c_ref, dt_ref, cw_ref, cb_ref, dtb_ref, alog_ref, dsk_ref, nrm_ref, exp_ref) = refs[:12]
    i = 12
    exl_ref = exp_ref
    if has_exl:
        exl_ref = refs[i]
        i += 1
    cs_ref = h0_ref = None
    if not zero_init:
        cs_ref, h0_ref = refs[i:i + 2]
        i += 2
    if has_prev:
        i += 1
    y_ref, h_out_ref, u, h_scr, xbd = refs[i:i + 5]
    t = pl.program_id(1)
    Tb = L * nc
    gw = hpg * P
    inner = G * gw
    wl = hpg * L
    nconv = cw_ref.shape[0]
    halo = SUBLANE

    direct = nt == 1 and nc == 1

    @pl.when(t == 0)
    def _init():
        xbd[...] = jnp.zeros_like(xbd)
        u[0:halo, :] = jnp.zeros((halo, u.shape[1]), F32)
        if not zero_init:
            u[halo - (nconv - 1):halo, :] = cs_ref[...]
        if not direct:
            if zero_init:
                h_scr[...] = jnp.zeros_like(h_scr)
            else:
                h_scr[...] = h0_ref[...].reshape(inner, N)

    def load_h(g):
        if not direct:
            return h_scr[g * gw:(g + 1) * gw, :]
        if zero_init:
            return jnp.zeros((gw, N), F32)
        return h0_ref[g * hpg:(g + 1) * hpg].reshape(gw, N)

    def store_h(g, val):
        if direct:
            h_out_ref[g * hpg:(g + 1) * hpg] = val.reshape(hpg, P, N)
        else:
            h_scr[g * gw:(g + 1) * gw, :] = val

    u[halo:halo + Tb, 0:inner] = x_ref[...]
    u[halo:halo + Tb, inner:inner + G * N] = b_ref[...]
    u[halo:halo + Tb, inner + G * N:inner + 2 * G * N] = c_ref[...]

    def conv(r0, lo, width):
        win = u[pl.ds(r0, L + halo), lo:lo + width]
        acc = cb_ref[:, lo:lo + width] + win[halo:, :] * cw_ref[nconv - 1:nconv, lo:lo + width]
        for j in range(nconv - 1):
            moved = pltpu.roll(win, nconv - 1 - j, 0)[halo:, :]
            acc = acc + moved * cw_ref[j:j + 1, lo:lo + width]
        return _silu(acc)

    dtp = _softplus(dt_ref[...] + dtb_ref[...])
    a_h = dtp * (-jnp.exp(alog_ref[...]))

    tri = jnp.where(_tri(L), 1.0, 0.0).astype(BF16)
    tri2 = jnp.concatenate([tri, tri], axis=1)
    ones2 = jnp.ones((2 * L, LANE), BF16)
    row_l = lax.broadcasted_iota(jnp.int32, (L, G * wl), 0)
    s_l = lax.broadcasted_iota(jnp.int32, (L, G * wl), 1) % L
    later = row_l > s_l
    causal = row_l >= s_l
    early_cast = L % 16 == 0
    lane_in_piece = lax.broadcasted_iota(jnp.int32, (L, LANE), 1)
    gc = [slice(g * gw, (g + 1) * gw) for g in range(G)]
    gbatch = 2 if L >= P else G
    assert G % gbatch == 0

    for c in range(nc):
        sl = slice(c * L, (c + 1) * L)
        da2 = _split_lanes(jnp.concatenate([dtp[sl], a_h[sl]], axis=0))
        da_exp = _dot(da2, exp_ref[...])
        dt_exp = da_exp[:L]
        a_exp_p = da_exp[L:]
        a_exp_l = _dot(_split_lanes(a_h[sl]), exl_ref[...]) if has_exl else a_exp_p
        ap2 = _split_rows(a_exp_p)
        al2 = _split_rows(jnp.where(later, a_exp_l, 0.0))
        ccol = _dot(tri2, ap2)
        seg = _dot(tri2, al2)
        decay = jnp.where(causal, jnp.exp(seg), 0.0)
        grow = jnp.exp(ccol)
        fade = jnp.exp(ccol[L - 1:L, :] - ccol)
        lp = gc
        ll = [slice(g * wl, (g + 1) * wl) for g in range(G)]

        for g0 in range(0, G, gbatch):
            batch = range(g0, g0 + gbatch)
            xs, bs, w_cats, inters = {}, {}, {}, {}
            for g in batch:
                x = conv(c * L, g * gw, gw)
                b = conv(c * L, inner + g * N, N)
                cm = conv(c * L, inner + G * N + g * N, N).astype(BF16)
                if early_cast:
                    b_t = jnp.concatenate([b.astype(BF16)] * hpg, axis=0)
                else:
                    b_t = jnp.concatenate([b] * hpg, axis=0).astype(BF16)
                w_cats[g] = (_dot_nt(cm, b_t) * decay[:, ll[g]]).astype(BF16)
                inters[g] = _dot_nt(cm, load_h(g).astype(BF16))
                xs[g] = x
                bs[g] = b

            xts, intras = {}, {}
            for g in batch:
                xt = xs[g] * dt_exp[:, lp[g]]
                xt_c = xt.astype(xbd.dtype)
                slab = g * nc + c
                for hh in range(hpg):
                    lo = (hh * P) // LANE * LANE
                    piece = xt_c[:, lo:lo + LANE]
                    own = (lane_in_piece >= hh * P - lo) & (lane_in_piece < (hh + 1) * P - lo)
                    xbd[slab, hh * L:(hh + 1) * L, lo:lo + LANE] = jnp.where(own, piece, jnp.zeros_like(piece))
                intras[g] = _dot(w_cats[g], xbd[slab].astype(BF16))
                xts[g] = xt

            for g in batch:
                x_dl = (xts[g] * fade[:, lp[g]]).astype(BF16)
                tot = _dot_tn(ap2[:, lp[g]], ones2)
                h = load_h(g)
                store_h(g, h * jnp.concatenate([jnp.exp(tot)] * (N // LANE), axis=1)
                        + _dot_tn(x_dl, bs[g].astype(BF16)))

            for g in batch:
                y = intras[g] + grow[:, lp[g]] * inters[g] + dsk_ref[:, gc[g]] * xs[g]
                yg = y * _silu(z_ref[sl, gc[g]])
                y_ref[sl, gc[g]] = (_rms(yg) * nrm_ref[:, gc[g]]).astype(y_ref.dtype)

    u[0:halo, :] = u[Tb:Tb + halo, :]

    if not direct:
        @pl.when(t == nt - 1)
        def _fin():
            h_out_ref[...] = h_scr[...].reshape(h_out_ref.shape)


def _ada(c_all, w_ada, b_ada, tn=1024):
    depth, d, n = w_ada.shape
    m = c_all.shape[0]
    return pl.pallas_call(
        _ada_kernel,
        grid=(depth, n // tn),
        in_specs=[
            pl.BlockSpec((m, d), lambda l, j: (0, 0)),
            pl.BlockSpec((None, d, tn), lambda l, j: (l, 0, j)),
            pl.BlockSpec((None, 1, tn), lambda l, j: (l, 0, j)),
        ],
        out_specs=pl.BlockSpec((None, m, tn), lambda l, j: (l, 0, j)),
        out_shape=jax.ShapeDtypeStruct((depth, m, n), F32),
        compiler_params=_cparams(("parallel", "parallel")),
        name="ada",
    )(c_all, w_ada, b_ada.reshape(depth, 1, n))


class _Rows:
    def __init__(self, n_seq, seq_len, long_seq, tm_max=1024):
        self.n_seq, self.seq_len, self.long_seq = n_seq, seq_len, long_seq
        self.m = n_seq * seq_len
        if long_seq:
            self.tm = min(tm_max, seq_len)
            assert seq_len % self.tm == 0
            self.reps = 1
            self.mod_rows = 1
        else:
            self.tm = min(self.m, tm_max)
            assert self.m % self.tm == 0 and self.tm % seq_len == 0
            self.reps = seq_len
            self.mod_rows = self.tm // seq_len
        self.n_tiles = self.m // self.tm

    def lhs_spec(self, width):
        if self.n_tiles == 1:
            return pl.BlockSpec((self.tm, width), lambda i, j: (i, 0), pipeline_mode=pl.Buffered(1))
        return pl.BlockSpec((self.tm, width), lambda i, j: (i, 0))

    def mod_array(self, mod):
        depth, _, n = mod.shape
        if self.reps == 1:
            return mod.reshape(depth, self.n_seq, 1, n)
        return mod

    def mod_spec(self, l, tn, col_block):
        if self.reps == 1:
            per = self.seq_len // self.tm
            return pl.BlockSpec((None, None, 1, tn), lambda i, j: (l, i // per, 0, col_block(j)))
        return pl.BlockSpec((None, self.mod_rows, tn), lambda i, j: (l, i, col_block(j)))


def _final_norm(x, gain, rows):
    m, d = x.shape
    return pl.pallas_call(
        _final_norm_kernel,
        grid=(rows.n_tiles,),
        in_specs=[pl.BlockSpec((rows.tm, d), lambda i: (i, 0)), pl.BlockSpec((1, d), lambda i: (0, 0))],
        out_specs=pl.BlockSpec((rows.tm, d), lambda i: (i, 0)),
        out_shape=jax.ShapeDtypeStruct((m, d), F32),
        compiler_params=_cparams(("parallel",)),
        name="final_norm",
    )(x, gain.reshape(1, d))


def _in_proj(x, mod_arr, w_t, l, rows, tn, v_scale, v_shift):
    m, d = x.shape
    n = w_t.shape[1]
    return pl.pallas_call(
        functools.partial(_norm_proj_kernel, reps=rows.reps),
        grid=(rows.n_tiles, n // tn),
        in_specs=[
            pl.BlockSpec((rows.tm, d), lambda i, j: (i, 0), pipeline_mode=pl.Buffered(1)),
            rows.mod_spec(l, d, lambda j: v_scale),
            rows.mod_spec(l, d, lambda j: v_shift),
            pl.BlockSpec((None, tn, d), lambda i, j: (l, j, 0)),
        ],
        out_specs=pl.BlockSpec((rows.tm, tn), lambda i, j: (i, j)),
        out_shape=jax.ShapeDtypeStruct((m, n), F32),
        scratch_shapes=[pltpu.VMEM((rows.tm, d), BF16)],
        compiler_params=_cparams(("parallel", "arbitrary")),
        name="in_proj",
    )(x, mod_arr, mod_arr, w_t)


def _merge(cfg, oa, yb, proj, wa, wb, l, rows):
    m = oa.shape[0]
    d = cfg.d_model
    tn = 512 if oa.dtype == BF16 else 256
    ga0 = cfg.off_gates // tn
    gb0 = (cfg.off_gates + d) // tn
    return pl.pallas_call(
        _merge_kernel,
        grid=(rows.n_tiles, d // tn),
        in_specs=[
            rows.lhs_spec(oa.shape[1]),
            rows.lhs_spec(yb.shape[1]),
            pl.BlockSpec((None, wa.shape[1], tn), lambda i, j: (l, 0, j)),
            pl.BlockSpec((None, wb.shape[1], tn), lambda i, j: (l, 0, j)),
            pl.BlockSpec((rows.tm, tn), lambda i, j: (i, ga0 + j)),
            pl.BlockSpec((rows.tm, tn), lambda i, j: (i, gb0 + j)),
        ],
        out_specs=pl.BlockSpec((rows.tm, tn), lambda i, j: (i, j)),
        out_shape=jax.ShapeDtypeStruct((m, d), BF16),
        compiler_params=_cparams(("parallel", "arbitrary")),
        name="merge",
    )(oa, yb, wa, wb, proj, proj)


def _resid(a, w, x, mod_arr, rows, l, v_gate, tn=512, name="resid"):
    m, kdim = a.shape
    d = x.shape[1]
    per = d // tn
    return pl.pallas_call(
        functools.partial(_resid_kernel, reps=rows.reps),
        grid=(rows.n_tiles, d // tn),
        in_specs=[
            rows.lhs_spec(kdim),
            pl.BlockSpec((None, kdim, tn), lambda i, j: (l, 0, j)),
            pl.BlockSpec((rows.tm, tn), lambda i, j: (i, j)),
            rows.mod_spec(l, tn, lambda j: v_gate * per + j),
        ],
        out_specs=pl.BlockSpec((rows.tm, tn), lambda i, j: (i, j)),
        out_shape=jax.ShapeDtypeStruct((m, d), F32),
        compiler_params=_cparams(("parallel", "arbitrary")),
        name=name,
    )(a, w, x, mod_arr)


def _ffn_in(x, mod_arr, w, l, rows, d_ff, v_scale, v_shift, tn=512):
    m, d = x.shape
    nb = d_ff // tn
    return pl.pallas_call(
        functools.partial(_norm_ffn_in_kernel, reps=rows.reps),
        grid=(rows.n_tiles, nb),
        in_specs=[
            rows.lhs_spec(d),
            rows.mod_spec(l, d, lambda j: v_scale),
            rows.mod_spec(l, d, lambda j: v_shift),
            pl.BlockSpec((None, d, tn), lambda i, j: (l, 0, j)),
            pl.BlockSpec((None, d, tn), lambda i, j: (l, 0, nb + j)),
        ],
        out_specs=pl.BlockSpec((rows.tm, tn), lambda i, j: (i, j)),
        out_shape=jax.ShapeDtypeStruct((m, d_ff), BF16),
        scratch_shapes=[pltpu.VMEM((rows.tm, d), BF16)],
        compiler_params=_cparams(("parallel", "arbitrary")),
        name="ffn_in",
    )(x, mod_arr, mod_arr, w, w)


def _scan_blocks(rows, chunk, max_chunks):
    L = min(chunk, rows.seq_len)
    assert rows.seq_len % L == 0
    n_chunks = rows.seq_len // L
    nc = min(max_chunks, n_chunks)
    assert n_chunks % nc == 0
    return L, nc, n_chunks // nc


def _gla(cfg, proj, l, rows, wg, bg, gn, state, prev, out_dtype):
    H, dk, dv = cfg.gla_heads, cfg.dk, cfg.dv
    hk, hv = H * dk, H * dv
    B = rows.n_seq
    L, nc, nt = _scan_blocks(rows, cfg.gla_chunk, 4)
    ns = 1
    if nt == 1 and nc == 1:
        ns = 4 if B % 4 == 0 else (2 if B % 2 == 0 else 1)
    Tb = ns * L * nc
    assert cfg.off_q % hk == 0 and cfg.off_v % hv == 0
    rb = lambda b, t: b * nt + t
    in_specs = [
        pl.BlockSpec((Tb, hk), lambda b, t: (rb(b, t), cfg.off_q // hk)),
        pl.BlockSpec((Tb, hk), lambda b, t: (rb(b, t), cfg.off_k // hk)),
        pl.BlockSpec((Tb, hv), lambda b, t: (rb(b, t), cfg.off_v // hv)),
        pl.BlockSpec((Tb, LANE), lambda b, t: (rb(b, t), cfg.off_glr // LANE)),
        pl.BlockSpec((Tb, hv), lambda b, t: (rb(b, t), cfg.off_r // hv)),
        pl.BlockSpec((None, LANE, hk), lambda b, t: (l, 0, 0)),
        pl.BlockSpec((None, 1, hk), lambda b, t: (l, 0, 0)),
        pl.BlockSpec((None, 1, dv), lambda b, t: (l, 0, 0)),
    ]
    args = [proj, proj, proj, proj, proj, wg, bg, gn]
    state_spec = pl.BlockSpec((None, ns, H, dk, dv), lambda b, t: (l, b, 0, 0, 0))
    if state is not None:
        in_specs.append(state_spec)
        args.append(state)
    aliases = {}
    if prev is not None:
        in_specs.append(pl.BlockSpec(memory_space=pl.ANY))
        aliases = {len(args): 1}
        args.append(prev)
    kern = functools.partial(_gla_kernel, L=L, nc=nc, nt=nt, ns=ns, H=H, zero_init=state is None,
                             has_prev=prev is not None, scale=dk ** -0.5, inv_tau=1.0 / cfg.gla_tau)
    return pl.pallas_call(
        kern,
        grid=(B // ns, nt),
        in_specs=in_specs,
        out_specs=[pl.BlockSpec((Tb, hv), lambda b, t: (rb(b, t), 0)), state_spec],
        out_shape=[
            jax.ShapeDtypeStruct((rows.m, hv), out_dtype),
            jax.ShapeDtypeStruct((cfg.depth, B, H, dk, dv), F32),
        ],
        scratch_shapes=[pltpu.VMEM((ns * H, dk, dv), F32)],
        input_output_aliases=aliases,
        compiler_params=_cparams(("parallel", "arbitrary")),
        name="gla_scan",
    )(*args)


def _head_expander(n_heads, per):
    col_head = jnp.arange(n_heads * per, dtype=jnp.int32) // per
    ex = (col_head[None, :] == jnp.arange(LANE, dtype=jnp.int32)[:, None]).astype(BF16)
    return jnp.concatenate([ex, ex], axis=0)


def _ssd(cfg, proj, l, rows, conv_w, conv_b, dtb, alog, dsk, nrm, conv_state, state, prev, out_dtype):
    G, gw, N, P, hpg = cfg.ssd_groups, cfg.gw, cfg.ssd_state, cfg.ssd_head_dim, cfg.hpg
    inner, heads, cdim = cfg.inner, cfg.ssd_heads, cfg.conv_dim
    B = rows.n_seq
    K = cfg.ssd_conv
    L, nc, nt = _scan_blocks(rows, cfg.ssd_chunk, 2)
    Tb = L * nc
    wl = hpg * L
    gn = G * N
    assert cfg.off_z % inner == 0 and cfg.off_xbc % inner == 0 and (cfg.off_xbc + inner) % gn == 0
    rb = lambda b, t: b * nt + t
    in_specs = [
        pl.BlockSpec((Tb, inner), lambda b, t: (rb(b, t), cfg.off_z // inner)),
        pl.BlockSpec((Tb, inner), lambda b, t: (rb(b, t), cfg.off_xbc // inner)),
        pl.BlockSpec((Tb, gn), lambda b, t: (rb(b, t), (cfg.off_xbc + inner) // gn)),
        pl.BlockSpec((Tb, gn), lambda b, t: (rb(b, t), (cfg.off_xbc + inner) // gn + 1)),
        pl.BlockSpec((Tb, LANE), lambda b, t: (rb(b, t), cfg.off_dt // LANE)),
        pl.BlockSpec((None, K, cdim), lambda b, t: (l, 0, 0)),
        pl.BlockSpec((None, 1, cdim), lambda b, t: (l, 0, 0)),
        pl.BlockSpec((None, 1, LANE), lambda b, t: (l, 0, 0)),
        pl.BlockSpec((None, 1, LANE), lambda b, t: (l, 0, 0)),
        pl.BlockSpec((None, 1, inner), lambda b, t: (l, 0, 0)),
        pl.BlockSpec((None, 1, inner), lambda b, t: (l, 0, 0)),
        pl.BlockSpec((2 * LANE, inner), lambda b, t: (0, 0)),
    ]
    args = [proj, proj, proj, proj, proj, conv_w, conv_b, dtb, alog, dsk, nrm, _head_expander(heads, P)]
    has_exl = L != P
    if has_exl:
        in_specs.append(pl.BlockSpec((2 * LANE, heads * L), lambda b, t: (0, 0)))
        args.append(_head_expander(heads, L))
    state_spec = pl.BlockSpec((None, None, heads, P, N), lambda b, t: (l, b, 0, 0, 0))
    if state is not None:
        in_specs += [pl.BlockSpec((None, None, K - 1, cdim), lambda b, t: (l, b, 0, 0)), state_spec]
        args += [conv_state, state]
    aliases = {}
    if prev is not None:
        in_specs.append(pl.BlockSpec(memory_space=pl.ANY))
        aliases = {len(args): 1}
        args.append(prev)
    kern = functools.partial(_ssd_kernel, L=L, nc=nc, nt=nt, G=G, hpg=hpg, P=P, N=N, zero_init=state is None,
                             has_prev=prev is not None, has_exl=has_exl)
    xbd_dtype = BF16 if L % 16 == 0 else F32
    return pl.pallas_call(
        kern,
        grid=(B, nt),
        in_specs=in_specs,
        out_specs=[pl.BlockSpec((Tb, inner), lambda b, t: (rb(b, t), 0)), state_spec],
        out_shape=[
            jax.ShapeDtypeStruct((rows.m, inner), out_dtype),
            jax.ShapeDtypeStruct((cfg.depth, B, heads, P, N), F32),
        ],
        scratch_shapes=[
            pltpu.VMEM((Tb + SUBLANE, cdim), F32),
            pltpu.VMEM((inner if nt * nc > 1 else SUBLANE, N), F32),
            pltpu.VMEM((G * nc, wl, gw), xbd_dtype),
        ],
        input_output_aliases=aliases,
        compiler_params=_cparams(("parallel", "arbitrary")),
        name="ssd_scan",
    )(*args)


def _pack_w_in(cfg, w_in, tn):
    H, dk, dv = cfg.gla_heads, cfg.dk, cfg.dv
    sizes = (H * dk, H * dk, H * dv, H * dv, cfg.gla_rank, cfg.inner, cfg.conv_dim, cfg.ssd_heads, 2 * cfg.d_model)
    offs = [0]
    for s in sizes:
        offs.append(offs[-1] + s)
    w_t = jnp.swapaxes(w_in, 1, 2)
    seg = lambda i: w_t[:, offs[i]:offs[i + 1], :].astype(BF16)
    padto = lambda a, n: jnp.pad(a, ((0, 0), (0, n - a.shape[1]), (0, 0)))
    n_p = cfg.n_packed(tn)
    parts = [seg(5), seg(6), seg(0), seg(1), seg(2), seg(3), seg(8), padto(seg(4), LANE), padto(seg(7), LANE)]
    return padto(jnp.concatenate(parts, axis=1), n_p)


def _trunk(cfg, x, mod, rows, states, weights, scan_dtype):
    (w_in_p, wg, bg, gn, w_gla_proj, conv_w, conv_b, dtb, alog, dsk, nrm, w_ssd_proj, w_mix_out,
     w_ffn_in, w_ffn_out, final_norm, tn_in) = weights
    s_gla, s_ssm, s_conv = states
    mod_arr = rows.mod_array(mod)
    rows_e = _Rows(rows.n_seq, rows.seq_len, rows.long_seq, tm_max=512)
    B = rows.n_seq
    K = cfg.ssd_conv
    new_gla = new_ssm = None
    new_conv = []
    for l in range(cfg.depth):
        proj = _in_proj(x, mod_arr, w_in_p, l, rows, 3 * tn_in if rows.long_seq else tn_in, 1, 0)
        oa, new_gla = _gla(cfg, proj, l, rows, wg, bg, gn, s_gla, new_gla, scan_dtype)
        yb, new_ssm = _ssd(cfg, proj, l, rows, conv_w, conv_b, dtb, alog, dsk, nrm, s_conv, s_ssm, new_ssm,
                           scan_dtype)
        assert rows.seq_len >= K - 1
        tail = proj.reshape(B, rows.seq_len, proj.shape[1])[:, rows.seq_len - (K - 1):, :]
        new_conv.append(tail[:, :, cfg.off_xbc:cfg.off_xbc + cfg.conv_dim])
        merged = _merge(cfg, oa, yb, proj, w_gla_proj, w_ssd_proj, l, rows)
        x = _resid(merged, w_mix_out, x, mod_arr, rows, l, 2, tn=1024, name="mix_out")
        act = _ffn_in(x, mod_arr, w_ffn_in, l, rows, cfg.d_ff, 4, 3)
        x = _resid(act, w_ffn_out, x, mod_arr, rows, l, 5, name="ffn_out")
    y = _final_norm(x, final_norm, rows_e)
    return y, new_gla, new_ssm, jnp.stack(new_conv)


def _forward(cfg, x_prompt, x_sample, c_prompt, c_sample, state_gla, state_ssm, state_conv,
             w_ada, b_ada, w_in, w_gla_gate, b_gla_gate, gla_norm, w_gla_proj, conv_w, conv_b,
             dt_bias, A_log, d_skip, ssd_norm, w_ssd_proj, w_mix_out, w_ffn_in, w_ffn_out, final_norm):
    depth, d = cfg.depth, cfg.d_model
    bp, tp, _ = x_prompt.shape
    bs, ts, _ = x_sample.shape
    H, dk, dv = cfg.gla_heads, cfg.dk, cfg.dv
    P = cfg.ssd_head_dim
    assert cfg.ssd_heads <= LANE and cfg.gla_rank <= LANE
    assert cfg.off_gates % 256 == 0

    pad = (-bp) % SUBLANE
    c_all = jnp.concatenate([c_prompt, jnp.zeros((pad, d), F32), c_sample], axis=0)
    mod = _ada(c_all, w_ada, b_ada)
    mod_p = mod[:, :bp]
    mod_s = mod[:, bp + pad:]

    tn_in = 768
    padl = lambda a: jnp.pad(a, ((0, 0), (0, LANE - a.shape[1])))
    weights = (
        _pack_w_in(cfg, w_in, tn_in),
        jnp.pad(w_gla_gate, ((0, 0), (0, LANE - cfg.gla_rank), (0, 0))).astype(BF16),
        b_gla_gate.reshape(depth, 1, H * dk),
        gla_norm.reshape(depth, 1, dv),
        w_gla_proj.astype(BF16),
        conv_w,
        conv_b.reshape(depth, 1, cfg.conv_dim),
        padl(dt_bias).reshape(depth, 1, LANE),
        padl(A_log).reshape(depth, 1, LANE),
        jnp.repeat(d_skip, P, axis=1).reshape(depth, 1, cfg.inner),
        ssd_norm.reshape(depth, 1, cfg.inner),
        w_ssd_proj.astype(BF16),
        w_mix_out.astype(BF16),
        w_ffn_in.astype(BF16),
        w_ffn_out.astype(BF16),
        final_norm,
        tn_in,
    )
    rows_p = _Rows(bp, tp, True)
    rows_s = _Rows(bs, ts, False)
    y_p, gla_p, ssm_p, conv_p = _trunk(cfg, x_prompt.reshape(bp * tp, d), mod_p, rows_p,
                                       (None, None, None), weights, BF16)
    states_s = (state_gla, state_ssm, state_conv)
    y_s, gla_s, ssm_s, conv_s = _trunk(cfg, x_sample.reshape(bs * ts, d), mod_s, rows_s,
                                       states_s, weights, F32)
    return (y_p.reshape(bp, tp, d), y_s.reshape(bs, ts, d), gla_p, ssm_p, conv_p, gla_s, ssm_s, conv_s)


def kernel(x_prompt, x_sample, c_prompt, c_sample, state_gla, state_ssm, state_conv, w_ada, b_ada, w_in,
           w_gla_gate, b_gla_gate, gla_norm, w_gla_proj, conv_w, conv_b, dt_bias, A_log, d_skip, ssd_norm,
           w_ssd_proj, w_mix_out, w_ffn_in, w_ffn_out, final_norm):
    cfg = Cfg()
    assert x_prompt.shape[2] == cfg.d_model
    return _forward(cfg, x_prompt, x_sample, c_prompt, c_sample, state_gla, state_ssm, state_conv,
                    w_ada, b_ada, w_in, w_gla_gate, b_gla_gate, gla_norm, w_gla_proj, conv_w, conv_b,
                    dt_bias, A_log, d_skip, ssd_norm, w_ssd_proj, w_mix_out, w_ffn_in, w_ffn_out, final_norm)
```
